```python
import math
import jax
import jax.numpy as jnp
from jax import lax
import numpy as np

D_MODEL = 1024
BATCH = 4
SEQ = 4096
DEPTH = 4

CHUNK = 64
EPS = 1e-6
A_HEADS = 8
A_HEAD_DIM = 64
A_WIDTH = A_HEADS * A_HEAD_DIM
A_PAST_CHUNKS = 8
A_BAND = A_PAST_CHUNKS + 1
A_MAX_REL = 128
B_HEADS = 4
B_HEAD_DIM = 128
B_WIDTH = B_HEADS * B_HEAD_DIM
CONV_K = 4
FFN_HIDDEN = ((8 * D_MODEL + 3 * 256 - 1) // (3 * 256)) * 256
N_MOD = 6
IN_SIZES = (A_WIDTH, A_WIDTH, A_WIDTH, 3 * B_WIDTH, B_WIDTH, B_HEADS, B_HEADS, D_MODEL, D_MODEL)
IN_DIM = sum(IN_SIZES)
IN_SPLITS = tuple(int(s) for s in np.cumsum(IN_SIZES)[:-1])

kernel_name = "hybrid_chunk_attn_gated_deltanet_adaln"


def rmsnorm(x, g):
    xf = x.astype(jnp.float32)
    y = xf * lax.rsqrt(jnp.mean(xf * xf, axis=-1, keepdims=True) + EPS)
    return (y * g.astype(jnp.float32)).astype(x.dtype)


def l2norm(t):
    return t * lax.rsqrt(jnp.sum(t * t, axis=-1, keepdims=True) + EPS)


def chunk_band_attention(q, k, v, rel_table):
    bsz, seq, h, dh = q.shape
    nc = seq // CHUNK
    band = A_BAND * CHUNK
    qc = q.reshape(bsz, nc, CHUNK, h, dh)
    pad = ((0, 0), (A_PAST_CHUNKS, 0), (0, 0), (0, 0), (0, 0))
    kp = jnp.pad(k.reshape(bsz, nc, CHUNK, h, dh), pad)
    vp = jnp.pad(v.reshape(bsz, nc, CHUNK, h, dh), pad)
    band_idx = jnp.arange(nc)[:, None] + jnp.arange(A_BAND)[None, :]
    kb = kp[:, band_idx].reshape(bsz, nc, band, h, dh)
    vb = vp[:, band_idx].reshape(bsz, nc, band, h, dh)
    s = jnp.einsum('bnqhd,bnkhd->bnhqk', qc, kb).astype(jnp.float32) * (dh ** -0.5)
    qpos = A_PAST_CHUNKS * CHUNK + jnp.arange(CHUNK)
    rel = jnp.clip(qpos[:, None] - jnp.arange(band)[None, :], -A_MAX_REL, A_MAX_REL) + A_MAX_REL
    bias = rel_table.astype(jnp.float32)[:, rel]
    key_chunk = jnp.arange(nc)[:, None] - A_PAST_CHUNKS + (jnp.arange(band) // CHUNK)[None, :]
    valid = (key_chunk >= 0)[None, :, None, None, :]
    s = jnp.where(valid, s + bias[None, None], -1e30)
    p = jax.nn.softmax(s, axis=-1).astype(v.dtype)
    o = jnp.einsum('bnhqk,bnkhd->bnqhd', p, vb)
    return o.reshape(bsz, seq, h * dh)


def causal_depthwise_conv(x, w):
    ch = x.shape[-1]
    return lax.conv_general_dilated(
        x, w[:, None, :], window_strides=(1,), padding=[(CONV_K - 1, 0)],
        dimension_numbers=('NWC', 'WIO', 'NWC'), feature_group_count=ch)


def gated_delta_rule_chunked(q, k, v, g, beta):
    bsz, seq, h, dk = q.shape
    dv = v.shape[-1]
    nc = seq // CHUNK
    to_c = lambda t: t.reshape(bsz, nc, CHUNK, h, t.shape[-1]).transpose(0, 1, 3, 2, 4)
    q, k, v = to_c(q), to_c(k), to_c(v)
    g = g.reshape(bsz, nc, CHUNK, h).transpose(0, 1, 3, 2)
    beta = beta.reshape(bsz, nc, CHUNK, h).transpose(0, 1, 3, 2)
    g_cum = jnp.cumsum(g, axis=-1)
    incl = jnp.tril(jnp.ones((CHUNK, CHUNK), dtype=bool))
    strict = jnp.tril(jnp.ones((CHUNK, CHUNK), dtype=bool), k=-1)
    diff = g_cum[..., :, None] - g_cum[..., None, :]
    decay = jnp.exp(jnp.where(incl, diff, -jnp.inf))
    kk = jnp.einsum('bnhid,bnhjd->bnhij', k, k)
    lower = jnp.where(strict, beta[..., :, None] * kk * decay, 0.0)
    rhs = jnp.concatenate([v * beta[..., None], k * (beta * jnp.exp(g_cum))[..., None]], axis=-1)
    sol = lax.linalg.triangular_solve(lower, rhs, left_side=True, lower=True, unit_diagonal=True)
    u_intra, k_cumdecay = sol[..., :dv], sol[..., dv:]
    qk = jnp.einsum('bnhid,bnhjd->bnhij', q, k) * decay
    q_dec = q * jnp.exp(g_cum)[..., None]
    k_dec = k * jnp.exp(g_cum[..., -1:] - g_cum)[..., None]
    g_last = jnp.exp(g_cum[..., -1])

    def step(state, xs):
        u_i, w_i, qk_i, qd_i, kd_i, gl_i = xs
        u = u_i - jnp.einsum('bhck,bhkv->bhcv', w_i, state)
        o = jnp.einsum('bhck,bhkv->bhcv', qd_i, state) + jnp.einsum('bhij,bhjv->bhiv', qk_i, u)
        state = state * gl_i[..., None, None] + jnp.einsum('bhck,bhcv->bhkv', kd_i, u)
        return state, o

    xs = tuple(jnp.moveaxis(t, 1, 0) for t in (u_intra, k_cumdecay, qk, q_dec, k_dec, g_last))
    s0 = jnp.zeros((bsz, h, dk, dv), jnp.float32)
    _, o = lax.scan(step, s0, xs)
    return o.transpose(1, 0, 3, 2, 4).reshape(bsz, seq, h, dv)


def gated_deltanet(qkv_raw, z, b_raw, a_raw, w_conv, a_log, dt_bias, norm_g):
    bsz, seq, _ = qkv_raw.shape
    qkv = jax.nn.silu(causal_depthwise_conv(qkv_raw, w_conv)).astype(jnp.float32)
    q, k, v = jnp.split(qkv, 3, axis=-1)
    q = l2norm(q.reshape(bsz, seq, B_HEADS, B_HEAD_DIM)) * (B_HEAD_DIM ** -0.5)
    k = l2norm(k.reshape(bsz, seq, B_HEADS, B_HEAD_DIM))
    v = v.reshape(bsz, seq, B_HEADS, B_HEAD_DIM)
    beta = jax.nn.sigmoid(b_raw.astype(jnp.float32))
    g = -jnp.exp(a_log.astype(jnp.float32)) * jax.nn.softplus(
        a_raw.astype(jnp.float32) + dt_bias.astype(jnp.float32))
    o = gated_delta_rule_chunked(q, k, v, g, beta)
    zf = z.astype(jnp.float32).reshape(bsz, seq, B_HEADS, B_HEAD_DIM)
    o = (o * lax.rsqrt(jnp.mean(o * o, axis=-1, keepdims=True) + EPS)
         * norm_g.astype(jnp.float32) * jax.nn.silu(zf))
    return o.reshape(bsz, seq, B_WIDTH).astype(qkv_raw.dtype)


def setup_inputs(seed: int = 0) -> dict:
    key = jax.random.key(seed)
    ks = jax.random.split(key, 18)
    f32 = jnp.float32

    def nrm(k, shape, scale):
        return jax.random.normal(k, shape, f32) * scale

    x = nrm(ks[0], (BATCH, SEQ, D_MODEL), 1.0)
    c = nrm(ks[1], (BATCH, D_MODEL), 1.0)
    w_ada = nrm(ks[2], (DEPTH, D_MODEL, N_MOD * D_MODEL), 0.5 * D_MODEL ** -0.5)
    b_ada = nrm(ks[3], (DEPTH, N_MOD * D_MODEL), 0.02)
    norm1_g = 1.0 + nrm(ks[4], (DEPTH, D_MODEL), 0.02)
    norm2_g = 1.0 + nrm(ks[5], (DEPTH, D_MODEL), 0.02)
    w_in = nrm(ks[6], (DEPTH, D_MODEL, IN_DIM), D_MODEL ** -0.5)
    rel_table = nrm(ks[7], (DEPTH, A_HEADS, 2 * A_MAX_REL + 1), 0.2)
    w_conv = nrm(ks[8], (DEPTH, CONV_K, 3 * B_WIDTH), CONV_K ** -0.5)
    a_log = jnp.log(jax.random.uniform(ks[9], (DEPTH, B_HEADS), f32, 1.0, 16.0))
    dt = jnp.exp(jax.random.uniform(ks[10], (DEPTH, B_HEADS), f32, math.log(1e-3), math.log(1e-1)))
    dt_bias = dt + jnp.log(-jnp.expm1(-dt))
    gdn_norm_g = 1.0 + nrm(ks[11], (DEPTH, B_HEAD_DIM), 0.02)
    w_branch_a = nrm(ks[12], (DEPTH, A_WIDTH, D_MODEL), A_WIDTH ** -0.5)
    w_branch_b = nrm(ks[13], (DEPTH, B_WIDTH, D_MODEL), B_WIDTH ** -0.5)
    w_out = nrm(ks[14], (DEPTH, D_MODEL, D_MODEL), D_MODEL ** -0.5)
    w_ffn_in = nrm(ks[15], (DEPTH, D_MODEL, 2 * FFN_HIDDEN), D_MODEL ** -0.5)
    w_ffn_out = nrm(ks[16], (DEPTH, FFN_HIDDEN, D_MODEL), FFN_HIDDEN ** -0.5)
    final_g = 1.0 + nrm(ks[17], (D_MODEL,), 0.02)
    return {"x": x, "c": c, "w_ada": w_ada, "b_ada": b_ada, "norm1_g": norm1_g,
            "norm2_g": norm2_g, "w_in": w_in, "rel_table": rel_table, "w_conv": w_conv,
            "a_log": a_log, "dt_bias": dt_bias, "gdn_norm_g": gdn_norm_g,
            "w_branch_a": w_branch_a, "w_branch_b": w_branch_b, "w_out": w_out,
            "w_ffn_in": w_ffn_in, "w_ffn_out": w_ffn_out, "final_g": final_g}


def reference(x, c, w_ada, b_ada, norm1_g, norm2_g, w_in, rel_table, w_conv, a_log, dt_bias,
              gdn_norm_g, w_branch_a, w_branch_b, w_out, w_ffn_in, w_ffn_out, final_g):
    bsz, seq, _ = x.shape
    cond = jax.nn.silu(c)
    for l in range(DEPTH):
        mod = (cond @ w_ada[l] + b_ada[l])[:, None, :]
        sh1, sc1, gt1, sh2, sc2, gt2 = jnp.split(mod, N_MOD, axis=-1)
        h = rmsnorm(x, norm1_g[l]) * (1.0 + sc1) + sh1
        proj = h @ w_in[l]
        qa, ka, va, qkvb, zb, bb, ab, ga, gb = jnp.split(proj, IN_SPLITS, axis=-1)
        ya = chunk_band_attention(
            qa.reshape(bsz, seq, A_HEADS, A_HEAD_DIM),
            ka.reshape(bsz, seq, A_HEADS, A_HEAD_DIM),
            va.reshape(bsz, seq, A_HEADS, A_HEAD_DIM), rel_table[l])
        yb = gated_deltanet(qkvb, zb, bb, ab, w_conv[l], a_log[l], dt_bias[l], gdn_norm_g[l])
        merged = (jax.nn.sigmoid(ga) * (ya @ w_branch_a[l])
                  + jax.nn.sigmoid(gb) * (yb @ w_branch_b[l]))
        x = x + gt1 * (merged @ w_out[l])
        h = rmsnorm(x, norm2_g[l]) * (1.0 + sc2) + sh2
        gate, up = jnp.split(h @ w_ffn_in[l], 2, axis=-1)
        x = x + gt2 * ((jax.nn.silu(gate) * up) @ w_ffn_out[l])
    return rmsnorm(x, final_g)
```

```python
import functools

import jax
import jax.numpy as jnp
from jax import lax
from jax.experimental import pallas as pl
from jax.experimental.pallas import tpu as pltpu

F32 = jnp.float32
BF16 = jnp.bfloat16

EPS = 1e-6
CHUNK = 64
A_HEADS = 8
A_HEAD_DIM = 64
A_WIDTH = A_HEADS * A_HEAD_DIM
A_PAST_CHUNKS = 8
A_BAND = (A_PAST_CHUNKS + 1) * CHUNK
A_MAX_REL = 128
B_HEADS = 4
B_HEAD_DIM = 128
B_WIDTH = B_HEADS * B_HEAD_DIM
CONV_K = 4
N_MOD = 6
LANES = 128
SUBLANES = 8
VMEM_LIMIT = 56 * 1024 * 1024

ROW_TILE = 512
FFN_CHUNK = 256
GDN_TILE = 256


def _params(*sem):
    return pltpu.CompilerParams(dimension_semantics=sem, vmem_limit_bytes=VMEM_LIMIT)


def _resident(block_shape, index_map):
    return pl.BlockSpec(block_shape, index_map, pipeline_mode=pl.Buffered(1))


def _mm(a, b):
    return jnp.dot(a.astype(BF16), b.astype(BF16), preferred_element_type=F32)


def _mm_nt(a, b):
    return lax.dot_general(a.astype(BF16), b.astype(BF16), (((1,), (1,)), ((), ())),
                           preferred_element_type=F32)


def _sigmoid(x):
    return 1.0 / (1.0 + jnp.exp(-x))


def _silu(x):
    return x * _sigmoid(x)


def _norm_mod(x, gain, shift, scale):
    ms = jnp.mean(x * x, axis=-1, keepdims=True)
    return x * lax.rsqrt(ms + EPS) * gain * (1.0 + scale) + shift


def _mod_kernel(c_ref, w_ref, b_ref, o_ref):
    cond = _silu(c_ref[...])
    o_ref[...] = _mm(cond, w_ref[...]) + b_ref[...]


def _modulation(c, w_ada, b_ada):
    depth, d, n = w_ada.shape
    bsz = c.shape[0]
    rows = -(-bsz // SUBLANES) * SUBLANES
    c_pad = jnp.zeros((rows, d), F32).at[:bsz].set(c)
    tn = n // 4
    out = pl.pallas_call(
        _mod_kernel,
        grid=(depth, n // tn),
        in_specs=[
            pl.BlockSpec((rows, d), lambda l, j: (0, 0)),
            pl.BlockSpec((None, d, tn), lambda l, j: (l, 0, j)),
            pl.BlockSpec((None, 1, tn), lambda l, j: (l, 0, j)),
        ],
        out_specs=pl.BlockSpec((None, rows, tn), lambda l, j: (l, 0, j)),
        out_shape=jax.ShapeDtypeStruct((depth, rows, n), F32),
        compiler_params=_params("arbitrary", "arbitrary"),
        name="adaln_modulation",
    )(c_pad, w_ada, b_ada.reshape(depth, 1, n))
    return out[:, :bsz].reshape(depth, bsz, N_MOD, d)


_IN_SEGS = (3 * A_WIDTH, 3 * B_WIDTH, B_WIDTH, None, LANES)


def _inproj_kernel(x_ref, mod_ref, g_ref, w_ref, oa_ref, ob_ref, oz_ref, og_ref, oba_ref):
    h = _norm_mod(x_ref[...], g_ref[...], mod_ref[0:1, :], mod_ref[1:2, :]).astype(BF16)
    col = 0
    for o_ref in (oa_ref, ob_ref, oz_ref, og_ref, oba_ref):
        n = o_ref.shape[-1]
        for c0 in range(0, n, 512):
            cw = min(512, n - c0)
            o_ref[:, c0:c0 + cw] = jnp.dot(
                h, w_ref[:, col + c0:col + c0 + cw], preferred_element_type=F32).astype(o_ref.dtype)
        col += n


def _in_projection(x2, mod_l, gain, w_cat, layer, rows_per_batch):
    m, d = x2.shape
    tm = min(ROW_TILE, rows_per_batch)
    tpb = rows_per_batch // tm
    n_tot = w_cat.shape[-1]
    widths = (3 * A_WIDTH, 3 * B_WIDTH, B_WIDTH, 2 * d, LANES)
    dtypes = (BF16, F32, F32, F32, F32)
    row_spec = lambda w: pl.BlockSpec((tm, w), lambda i: (i, 0))
    return pl.pallas_call(
        _inproj_kernel,
        grid=(m // tm,),
        in_specs=[
            row_spec(d),
            pl.BlockSpec((None, N_MOD, d), lambda i: (i // tpb, 0, 0)),
            _resident((None, 1, d), lambda i: (layer, 0, 0)),
            _resident((None, d, n_tot), lambda i: (layer, 0, 0)),
        ],
        out_specs=[row_spec(w) for w in widths],
        out_shape=[jax.ShapeDtypeStruct((m, w), t) for w, t in zip(widths, dtypes)],
        compiler_params=_params("arbitrary"),
        name="in_projection",
    )(x2, mod_l, gain, w_cat)


def _attn_kernel(q_ref, kp_ref, kc_ref, vp_ref, vc_ref, bias_ref, o_ref, k_scr, v_scr):
    tq = q_ref.shape[0]
    first = pl.program_id(1) == 0
    k_scr[0:tq, :] = kp_ref[...]
    k_scr[tq:2 * tq, :] = kc_ref[...]
    v_scr[0:tq, :] = vp_ref[...]
    v_scr[tq:2 * tq, :] = vc_ref[...]
    lane = lax.broadcasted_iota(jnp.int32, (CHUNK, LANES), 1)
    col = lax.broadcasted_iota(jnp.int32, (CHUNK, A_BAND), 1)
    scale = A_HEAD_DIM ** -0.5
    past = A_PAST_CHUNKS * CHUNK

    def chunk_body(qi, carry):
        r0 = pl.multiple_of(qi * CHUNK, CHUNK)
        base = pl.multiple_of(r0 + (tq - past), CHUNK)
        valid = jnp.logical_or(col + base >= tq, jnp.logical_not(first))
        for p in range(A_WIDTH // LANES):
            cs = slice(p * LANES, (p + 1) * LANES)
            q2 = q_ref[pl.ds(r0, CHUNK), cs]
            kb = k_scr[pl.ds(base, A_BAND), cs]
            vb = v_scr[pl.ds(base, A_BAND), cs]
            outs = []
            for hh in range(LANES // A_HEAD_DIM):
                in_head = (lane >= hh * A_HEAD_DIM) & (lane < (hh + 1) * A_HEAD_DIM)
                qm = jnp.where(in_head, q2, jnp.zeros_like(q2))
                s = _mm_nt(qm, kb) * scale + bias_ref[p * (LANES // A_HEAD_DIM) + hh]
                s = jnp.where(valid, s, -1e30)
                e = jnp.exp(s - jnp.max(s, axis=-1, keepdims=True))
                denom = jnp.sum(e, axis=-1, keepdims=True)
                outs.append(_mm(e, vb) / denom)
            o2 = jnp.where(lane < A_HEAD_DIM, outs[0], outs[1])
            o_ref[pl.ds(r0, CHUNK), cs] = o2.astype(o_ref.dtype)
        return carry

    lax.fori_loop(0, tq // CHUNK, chunk_body, 0)


def _attention(qkv, bias, layer, bsz, seq):
    tq = A_PAST_CHUNKS * CHUNK
    nblk = seq // tq
    cur = lambda c: (lambda b, i: (b * nblk + i, c))
    prev = lambda c: (lambda b, i: (b * nblk + jnp.maximum(i - 1, 0), c))
    blk = lambda f: pl.BlockSpec((tq, A_WIDTH), f)
    return pl.pallas_call(
        _attn_kernel,
        grid=(bsz, nblk),
        in_specs=[blk(cur(0)), blk(prev(1)), blk(cur(1)), blk(prev(2)), blk(cur(2)),
                  _resident((None, A_HEADS, CHUNK, A_BAND), lambda b, i: (layer, 0, 0, 0))],
        out_specs=blk(cur(0)),
        out_shape=jax.ShapeDtypeStruct((bsz * seq, A_WIDTH), BF16),
        scratch_shapes=[pltpu.VMEM((2 * tq, A_WIDTH), BF16), pltpu.VMEM((2 * tq, A_WIDTH), BF16)],
        compiler_params=_params("arbitrary", "arbitrary"),
        name="chunk_band_attention",
    )(qkv, qkv, qkv, qkv, qkv, bias)


def _unit_lower_inverse(a):
    n = a.shape[0]
    row = lax.broadcasted_iota(jnp.int32, (n, n), 0)
    col = lax.broadcasted_iota(jnp.int32, (n, n), 1)
    t = jnp.where(row == col, 1.0, 0.0).astype(F32) - a
    p = a
    for _ in range(n.bit_length() - 2):
        p = _mm(p, p)
        t = t + _mm(t, p)
    return t


def _gdn_kernel(x_ref, z_ref, ba_ref, wc_ref, hp_ref, ng_ref, o_ref, xext, qkv, carry, state):
    tb = x_ref.shape[0]
    width = x_ref.shape[1]
    first = pl.program_id(1) == 0

    @pl.when(first)
    def _():
        carry[...] = jnp.zeros_like(carry)
        state[...] = jnp.zeros_like(state)

    xext[0:SUBLANES, :] = carry[...]
    xext[SUBLANES:SUBLANES + tb, :] = x_ref[...]
    carry[...] = x_ref[tb - SUBLANES:tb, :]
    for r0 in range(0, tb, CHUNK):
        for c0 in range(0, width, 512):
            acc = jnp.zeros((CHUNK, 512), F32)
            for j in range(CONV_K):
                off = SUBLANES - (CONV_K - 1) + j
                acc = acc + wc_ref[j:j + 1, c0:c0 + 512] * xext[r0 + off:r0 + off + CHUNK, c0:c0 + 512]
            qkv[r0:r0 + CHUNK, c0:c0 + 512] = _silu(acc)

    row = lax.broadcasted_iota(jnp.int32, (CHUNK, CHUNK), 0)
    col = lax.broadcasted_iota(jnp.int32, (CHUNK, CHUNK), 1)
    incl = row >= col
    strict = row > col
    tril = jnp.where(incl, 1.0, 0.0).astype(F32)
    neg_decay_rate = -jnp.exp(hp_ref[0:1, :])
    dt_bias = hp_ref[1:2, :]

    def chunk_body(ci, c):
        r0 = pl.multiple_of(ci * CHUNK, CHUNK)
        ba = ba_ref[pl.ds(r0, CHUNK), :]
        beta_all = _sigmoid(ba)
        sp = ba + dt_bias
        softplus = jnp.maximum(sp, 0.0) + jnp.log(1.0 + jnp.exp(-jnp.abs(sp)))
        g_all = neg_decay_rate * softplus
        for h in range(B_HEADS):
            cs = slice(h * B_HEAD_DIM, (h + 1) * B_HEAD_DIM)
            q = qkv[pl.ds(r0, CHUNK), h * B_HEAD_DIM:(h + 1) * B_HEAD_DIM]
            k = qkv[pl.ds(r0, CHUNK), B_WIDTH + h * B_HEAD_DIM:B_WIDTH + (h + 1) * B_HEAD_DIM]
            v = qkv[pl.ds(r0, CHUNK), 2 * B_WIDTH + h * B_HEAD_DIM:2 * B_WIDTH + (h + 1) * B_HEAD_DIM]
            q = q * lax.rsqrt(jnp.sum(q * q, axis=-1, keepdims=True) + EPS) * (B_HEAD_DIM ** -0.5)
            k = k * lax.rsqrt(jnp.sum(k * k, axis=-1, keepdims=True) + EPS)
            beta = beta_all[:, h:h + 1]
            g = g_all[:, B_HEADS + h:B_HEADS + h + 1]
            gmat = jnp.broadcast_to(g, (CHUNK, CHUNK))
            gc_col = jnp.dot(tril, gmat, preferred_element_type=F32,
                             precision=lax.Precision.HIGHEST)
            gc_row = jnp.sum(jnp.where(row <= col, gmat, 0.0), axis=0, keepdims=True)
            gc = gc_col[:, 0:1]
            gc_last = gc_row[:, CHUNK - 1:CHUNK]
            decay = jnp.exp(jnp.where(incl, gc_col - gc_row, -jnp.inf))
            kk = _mm_nt(k, k)
            t = _unit_lower_inverse(jnp.where(strict, beta * kk * decay, 0.0))
            e_gc = jnp.exp(gc)
            rhs = jnp.concatenate([v * beta, k * (beta * e_gc)], axis=-1)
            sol = _mm(t, rhs)
            u_intra, w = sol[:, :B_HEAD_DIM], sol[:, B_HEAD_DIM:]
            qk = _mm_nt(q, k) * decay
            q_dec = q * e_gc
            k_dec = k * jnp.exp(gc_last - gc)
            s = state[h]
            u = u_intra - _mm(w, s)
            o = _mm(q_dec, s) + _mm(qk, u)
            state[h] = s * jnp.exp(gc_last) + _mm(k_dec.T, u)
            z = z_ref[pl.ds(r0, CHUNK), cs]
            o = o * lax.rsqrt(jnp.mean(o * o, axis=-1, keepdims=True) + EPS) * ng_ref[...] * _silu(z)
            o_ref[pl.ds(r0, CHUNK), cs] = o.astype(o_ref.dtype)
        return c

    lax.fori_loop(0, tb // CHUNK, chunk_body, 0)


def _gated_deltanet(qkv_raw, z, ba, w_conv, head_params, norm_g, layer, bsz, seq):
    tb = min(GDN_TILE, seq)
    nblk = seq // tb
    width = 3 * B_WIDTH
    blk = lambda w: pl.BlockSpec((tb, w), lambda b, i: (b * nblk + i, 0))
    return pl.pallas_call(
        _gdn_kernel,
        grid=(bsz, nblk),
        in_specs=[blk(width), blk(B_WIDTH), blk(LANES),
                  _resident((None, CONV_K, width), lambda b, i: (layer, 0, 0)),
                  _resident((None, 2, LANES), lambda b, i: (layer, 0, 0)),
                  _resident((None, 1, B_HEAD_DIM), lambda b, i: (layer, 0, 0))],
        out_specs=blk(B_WIDTH),
        out_shape=jax.ShapeDtypeStruct((bsz * seq, B_WIDTH), BF16),
        scratch_shapes=[pltpu.VMEM((tb + SUBLANES, width), F32), pltpu.VMEM((tb, width), F32),
                        pltpu.VMEM((SUBLANES, width), F32),
                        pltpu.VMEM((B_HEADS, B_HEAD_DIM, B_HEAD_DIM), F32)],
        compiler_params=_params("arbitrary", "arbitrary"),
        name="gated_deltanet",
    )(qkv_raw, z, ba, w_conv, head_params, norm_g)


def _merge_kernel(x_ref, ya_ref, yb_ref, g_ref, mod_ref, wa_ref, wb_ref, wo_ref, o_ref):
    d = x_ref.shape[-1]
    merged = (_sigmoid(g_ref[:, 0:d]) * jnp.dot(ya_ref[...], wa_ref[...], preferred_element_type=F32)
              + _sigmoid(g_ref[:, d:2 * d]) * jnp.dot(yb_ref[...], wb_ref[...], preferred_element_type=F32))
    o_ref[...] = x_ref[...] + mod_ref[2:3, :] * _mm(merged, wo_ref[...])


def _merge(x2, ya, yb, gates, mod_l, wa, wb, wo, layer, rows_per_batch):
    m, d = x2.shape
    tm = min(ROW_TILE, rows_per_batch)
    tpb = rows_per_batch // tm
    row_spec = lambda w: pl.BlockSpec((tm, w), lambda i: (i, 0))
    return pl.pallas_call(
        _merge_kernel,
        grid=(m // tm,),
        in_specs=[row_spec(d), row_spec(A_WIDTH), row_spec(B_WIDTH), row_spec(2 * d),
                  pl.BlockSpec((None, N_MOD, d), lambda i: (i // tpb, 0, 0)),
                  _resident((None, A_WIDTH, d), lambda i: (layer, 0, 0)),
                  _resident((None, B_WIDTH, d), lambda i: (layer, 0, 0)),
                  _resident((None, d, d), lambda i: (layer, 0, 0))],
        out_specs=row_spec(d),
        out_shape=jax.ShapeDtypeStruct((m, d), F32),
        compiler_params=_params("arbitrary"),
        name="merge_out_projection",
    )(x2, ya, yb, gates, mod_l, wa, wb, wo)


def _ffn_kernel(x_ref, mod_ref, g_ref, wi_ref, wo_ref, o_ref, acc_ref):
    x = x_ref[...]
    h = _norm_mod(x, g_ref[...], mod_ref[3:4, :], mod_ref[4:5, :]).astype(BF16)
    hidden = wo_ref.shape[0]
    for j in range(hidden // FFN_CHUNK):
        gu = jnp.dot(h, wi_ref[:, 2 * j * FFN_CHUNK:2 * (j + 1) * FFN_CHUNK], preferred_element_type=F32)
        act = (_silu(gu[:, :FFN_CHUNK]) * gu[:, FFN_CHUNK:]).astype(BF16)
        part = jnp.dot(act, wo_ref[j * FFN_CHUNK:(j + 1) * FFN_CHUNK, :], preferred_element_type=F32)
        if j == 0:
            acc_ref[...] = part
        else:
            acc_ref[...] += part
    o_ref[...] = x + mod_ref[5:6, :] * acc_ref[...]


def _ffn(x2, mod_l, gain, wi, wo, layer, rows_per_batch):
    m, d = x2.shape
    tm = min(ROW_TILE, rows_per_batch)
    tpb = rows_per_batch // tm
    hidden = wo.shape[1]
    row_spec = pl.BlockSpec((tm, d), lambda i: (i, 0))
    return pl.pallas_call(
        _ffn_kernel,
        grid=(m // tm,),
        in_specs=[row_spec,
                  pl.BlockSpec((None, N_MOD, d), lambda i: (i // tpb, 0, 0)),
                  _resident((None, 1, d), lambda i: (layer, 0, 0)),
                  _resident((None, d, 2 * hidden), lambda i: (layer, 0, 0)),
                  _resident((None, hidden, d), lambda i: (layer, 0, 0))],
        out_specs=row_spec,
        out_shape=jax.ShapeDtypeStruct((m, d), F32),
        scratch_shapes=[pltpu.VMEM((tm, d), F32)],
        compiler_params=_params("arbitrary"),
        name="swiglu_ffn",
    )(x2, mod_l, gain, wi, wo)


def _final_norm_kernel(x_ref, g_ref, o_ref):
    x = x_ref[...]
    o_ref[...] = x * lax.rsqrt(jnp.mean(x * x, axis=-1, keepdims=True) + EPS) * g_ref[...]


def _final_norm(x2, gain):
    m, d = x2.shape
    tm = min(ROW_TILE, m)
    return pl.pallas_call(
        _final_norm_kernel,
        grid=(m // tm,),
        in_specs=[pl.BlockSpec((tm, d), lambda i: (i, 0)), pl.BlockSpec((1, d), lambda i: (0, 0))],
        out_specs=pl.BlockSpec((tm, d), lambda i: (i, 0)),
        out_shape=jax.ShapeDtypeStruct((m, d), F32),
        compiler_params=_params("arbitrary"),
        name="final_rmsnorm",
    )(x2, gain)


def _prepare_in_weight(w_in, d):
    a3, b3 = 3 * A_WIDTH, 3 * B_WIDTH
    o_z = a3 + b3
    o_ba = o_z + B_WIDTH
    o_g = o_ba + 2 * B_HEADS
    pad = jnp.zeros(w_in.shape[:2] + (LANES - 2 * B_HEADS,), w_in.dtype)
    return jnp.concatenate(
        [w_in[..., :o_ba], w_in[..., o_g:o_g + 2 * d], w_in[..., o_ba:o_g], pad], axis=-1).astype(BF16)


def _prepare_ffn_in_weight(w_ffn_in):
    depth, d, two_h = w_ffn_in.shape
    hidden = two_h // 2
    w = w_ffn_in.reshape(depth, d, 2, hidden // FFN_CHUNK, FFN_CHUNK)
    return jnp.transpose(w, (0, 1, 3, 2, 4)).reshape(depth, d, two_h).astype(BF16)


def _rel_bias(rel_table):
    qpos = A_PAST_CHUNKS * CHUNK + jnp.arange(CHUNK)
    rel = jnp.clip(qpos[:, None] - jnp.arange(A_BAND)[None, :], -A_MAX_REL, A_MAX_REL) + A_MAX_REL
    return rel_table.astype(F32)[:, :, rel]


def kernel(x, c, w_ada, b_ada, norm1_g, norm2_g, w_in, rel_table, w_conv, a_log, dt_bias, gdn_norm_g,
           w_branch_a, w_branch_b, w_out, w_ffn_in, w_ffn_out, final_g):
    bsz, seq, d = x.shape
    depth = w_ada.shape[0]
    assert seq % (A_PAST_CHUNKS * CHUNK) == 0 and d % LANES == 0

    mod = _modulation(c, w_ada, b_ada)
    w_cat = _prepare_in_weight(w_in, d)
    wi = _prepare_ffn_in_weight(w_ffn_in)
    wo_ffn = w_ffn_out.astype(BF16)
    wa, wb, wo = w_branch_a.astype(BF16), w_branch_b.astype(BF16), w_out.astype(BF16)
    bias = _rel_bias(rel_table)
    lane_pad = jnp.zeros((depth, LANES - 2 * B_HEADS), F32)
    zeros_h = jnp.zeros((depth, B_HEADS), F32)
    head_params = jnp.stack([jnp.concatenate([zeros_h, a_log, lane_pad], axis=-1),
                             jnp.concatenate([zeros_h, dt_bias, lane_pad], axis=-1)], axis=1)
    g1 = norm1_g.reshape(depth, 1, d)
    g2 = norm2_g.reshape(depth, 1, d)
    ng = gdn_norm_g.reshape(depth, 1, B_HEAD_DIM)

    x2 = x.reshape(bsz * seq, d)
    for l in range(depth):
        qkv_a, qkv_b, z_b, gates, ba = _in_projection(x2, mod[l], g1, w_cat, l, seq)
        ya = _attention(qkv_a, bias, l, bsz, seq)
        yb = _gated_deltanet(qkv_b, z_b, ba, w_conv, head_params, ng, l, bsz, seq)
        x2 = _merge(x2, ya, yb, gates, mod[l], wa, wb, wo, l, seq)
        x2 = _ffn(x2, mod[l], g2, wi, wo_ffn, l, seq)
    return _final_norm(x2, final_g.reshape(1, d)).reshape(bsz, seq, d)
```

```python
import functools

import jax
import jax.numpy as jnp
from jax import lax
from jax.experimental import pallas as pl
from jax.experimental.pallas import tpu as pltpu

F32 = jnp.float32
BF16 = jnp.bfloat16

EPS = 1e-6
CHUNK = 64
A_HEADS = 8
A_HEAD_DIM = 64
A_WIDTH = A_HEADS * A_HEAD_DIM
A_PAST_CHUNKS = 8
A_BAND = (A_PAST_CHUNKS + 1) * CHUNK
A_MAX_REL = 128
B_HEADS = 4
B_HEAD_DIM = 128
B_WIDTH = B_HEADS * B_HEAD_DIM
CONV_K = 4
N_MOD = 6
LANES = 128
SUBLANES = 8
VMEM_LIMIT = 56 * 1024 * 1024

ROW_TILE = 512
COL_TILE = 512
FFN_CHUNK = 256
GDN_TILE = 128
ATTN_CHUNKS_PER_ITER = 2
BIAS_ROW = 5 * LANES


def _params(*sem):
    return pltpu.CompilerParams(dimension_semantics=sem, vmem_limit_bytes=VMEM_LIMIT)


def _resident(block_shape, index_map):
    return pl.BlockSpec(block_shape, index_map, pipeline_mode=pl.Buffered(1))


def _mm(a, b):
    return jnp.dot(a.astype(BF16), b.astype(BF16), preferred_element_type=F32)


def _mm_nt(a, b):
    return lax.dot_general(a.astype(BF16), b.astype(BF16), (((1,), (1,)), ((), ())),
                           preferred_element_type=F32)


def _rows(x, g, n=CHUNK):
    return x[g * n:(g + 1) * n]


def _bmm(a, b, groups, rows_a=CHUNK, rows_b=CHUNK):
    return jnp.concatenate([_mm(_rows(a, g, rows_a), _rows(b, g, rows_b)) for g in range(groups)], axis=0)


def _sigmoid(x):
    return 1.0 / (1.0 + jnp.exp(-x))


def _silu(x):
    return x * _sigmoid(x)


def _norm_mod(x, gain, shift, scale):
    ms = jnp.mean(x * x, axis=-1, keepdims=True)
    return x * lax.rsqrt(ms + EPS) * gain * (1.0 + scale) + shift


def _mod_kernel(c_ref, w_ref, b_ref, o_ref):
    cond = _silu(c_ref[...])
    o_ref[...] = _mm(cond, w_ref[...]) + b_ref[...]


def _modulation(c, w_ada, b_ada):
    depth, d, n = w_ada.shape
    bsz = c.shape[0]
    rows = -(-bsz // SUBLANES) * SUBLANES
    c_pad = jnp.zeros((rows, d), F32).at[:bsz].set(c)
    tn = n // 4
    out = pl.pallas_call(
        _mod_kernel,
        grid=(depth, n // tn),
        in_specs=[
            pl.BlockSpec((rows, d), lambda l, j: (0, 0)),
            pl.BlockSpec((None, d, tn), lambda l, j: (l, 0, j)),
            pl.BlockSpec((None, 1, tn), lambda l, j: (l, 0, j)),
        ],
        out_specs=pl.BlockSpec((None, rows, tn), lambda l, j: (l, 0, j)),
        out_shape=jax.ShapeDtypeStruct((depth, rows, n), F32),
        compiler_params=_params("arbitrary", "arbitrary"),
        name="adaln_modulation",
    )(c_pad, w_ada, b_ada.reshape(depth, 1, n))
    return out[:, :bsz].reshape(depth, bsz, N_MOD, d)


def _inproj_kernel(x_ref, mod_ref, g_ref, wm_ref, wg_ref, wba_ref, wc_ref,
                   oa_ref, ob_ref, oz_ref, og_ref, oba_ref, xext, *, tiles_per_batch):
    tm = x_ref.shape[0]
    a3, b3 = oa_ref.shape[1], ob_ref.shape[1]
    h = _norm_mod(x_ref[...], g_ref[...], mod_ref[0:1, :], mod_ref[1:2, :]).astype(BF16)

    def proj(w_ref, c0, cw):
        return jnp.dot(h, w_ref[:, c0:c0 + cw], preferred_element_type=F32)

    @pl.when(pl.program_id(0) % tiles_per_batch == 0)
    def _():
        xext[0:SUBLANES, :] = jnp.zeros((SUBLANES, b3), F32)

    for c0 in range(0, b3, COL_TILE):
        xext[SUBLANES:SUBLANES + tm, c0:c0 + COL_TILE] = proj(wm_ref, a3 + c0, COL_TILE)

    def conv_tile(c0):
        full = xext[:, c0:c0 + COL_TILE]
        acc = wc_ref[CONV_K - 1:CONV_K, c0:c0 + COL_TILE] * full[SUBLANES:]
        for s in range(1, CONV_K):
            shifted = pltpu.roll(full, s, 0)[SUBLANES:]
            acc = acc + wc_ref[CONV_K - 1 - s:CONV_K - s, c0:c0 + COL_TILE] * shifted
        ob_ref[:, c0:c0 + COL_TILE] = _silu(acc)

    others = [(oa_ref, wm_ref, c0, c0) for c0 in range(0, a3, COL_TILE)]
    others += [(og_ref, wg_ref, c0, c0) for c0 in range(0, og_ref.shape[1], COL_TILE)]
    others += [(oz_ref, wm_ref, a3 + b3, 0), (oba_ref, wba_ref, 0, 0)]
    conv_tiles = list(range(0, b3, COL_TILE))
    per_conv = -(-len(others) // len(conv_tiles))
    for n, c0 in enumerate(conv_tiles):
        conv_tile(c0)
        for o_ref, w_ref, wc0, oc0 in others[n * per_conv:(n + 1) * per_conv]:
            cw = min(COL_TILE, o_ref.shape[1])
            o_ref[:, oc0:oc0 + cw] = proj(w_ref, wc0, cw).astype(o_ref.dtype)
    xext[0:SUBLANES, :] = xext[tm:tm + SUBLANES, :]


def _in_projection(x2, mod_l, gain, w_main, w_gates, w_ba, w_conv, layer, rows_per_batch):
    m, d = x2.shape
    tm = min(ROW_TILE, rows_per_batch)
    tpb = rows_per_batch // tm
    widths = (3 * A_WIDTH, 3 * B_WIDTH, B_WIDTH, 2 * d, LANES)
    dtypes = (BF16, F32, F32, F32, F32)
    row_spec = lambda w: pl.BlockSpec((tm, w), lambda i: (i, 0))
    return pl.pallas_call(
        functools.partial(_inproj_kernel, tiles_per_batch=tpb),
        grid=(m // tm,),
        in_specs=[
            row_spec(d),
            pl.BlockSpec((None, N_MOD, d), lambda i: (i // tpb, 0, 0)),
            _resident((None, 1, d), lambda i: (layer, 0, 0)),
            _resident((None, d, w_main.shape[-1]), lambda i: (layer, 0, 0)),
            _resident((None, d, 2 * d), lambda i: (layer, 0, 0)),
            _resident((None, d, LANES), lambda i: (layer, 0, 0)),
            _resident((None, CONV_K, 3 * B_WIDTH), lambda i: (layer, 0, 0)),
        ],
        out_specs=[row_spec(w) for w in widths],
        out_shape=[jax.ShapeDtypeStruct((m, w), t) for w, t in zip(widths, dtypes)],
        scratch_shapes=[pltpu.VMEM((tm + SUBLANES, 3 * B_WIDTH), F32)],
        compiler_params=_params("arbitrary"),
        name="in_projection",
    )(x2, mod_l, gain, w_main, w_gates, w_ba, w_conv)


def _attn_kernel(q_ref, kp_ref, kc_ref, vp_ref, vc_ref, brow_ref, o_ref, k_scr, v_scr, bias_scr):
    tq = q_ref.shape[0]
    first = pl.program_id(1) == 0
    pairs = A_WIDTH // LANES
    heads_per_pair = LANES // A_HEAD_DIM

    @pl.when(jnp.logical_and(pl.program_id(0) == 0, first))
    def _():
        for h in range(A_HEADS):
            tiled = jnp.broadcast_to(brow_ref[h:h + 1, :], (CHUNK, BIAS_ROW))
            rolled = pltpu.roll(tiled, BIAS_ROW - (CHUNK - 1), 1, stride=1, stride_axis=0)
            bias_scr[h * CHUNK:(h + 1) * CHUNK, :] = rolled[:, :A_BAND]

    k_scr[0:tq, :] = kp_ref[...]
    k_scr[tq:2 * tq, :] = kc_ref[...]
    v_scr[0:tq, :] = vp_ref[...]
    v_scr[tq:2 * tq, :] = vc_ref[...]
    lane = lax.broadcasted_iota(jnp.int32, (CHUNK, LANES), 1)
    col = lax.broadcasted_iota(jnp.int32, (1, A_BAND), 1)
    scale = A_HEAD_DIM ** -0.5
    past = A_PAST_CHUNKS * CHUNK
    nq = ATTN_CHUNKS_PER_ITER

    def body(it, carry):
        r0s = [pl.multiple_of((it * nq + u) * CHUNK, CHUNK) for u in range(nq)]
        bases = [pl.multiple_of(r0 + (tq - past), CHUNK) for r0 in r0s]
        slabs = []
        for u in range(nq):
            parts = []
            for p in range(pairs):
                cs = slice(p * LANES, (p + 1) * LANES)
                q2 = q_ref[pl.ds(r0s[u], CHUNK), cs] * scale
                zero = jnp.zeros_like(q2)
                qs = jnp.concatenate(
                    [jnp.where((lane >= hh * A_HEAD_DIM) & (lane < (hh + 1) * A_HEAD_DIM), q2, zero)
                     for hh in range(heads_per_pair)], axis=0)
                parts.append(_mm_nt(qs, k_scr[pl.ds(bases[u], A_BAND), cs]))
            valid = jnp.logical_or(col + bases[u] >= tq, jnp.logical_not(first))
            slabs.append(jnp.concatenate(parts, axis=0) + bias_scr[...] + jnp.where(valid, 0.0, -1e30))
        s = jnp.concatenate(slabs, axis=0)
        e = jnp.exp(s - jnp.max(s, axis=-1, keepdims=True))
        rinv = 1.0 / jnp.sum(e, axis=-1, keepdims=True)
        pb = e.astype(BF16)
        for u in range(nq):
            for p in range(pairs):
                cs = slice(p * LANES, (p + 1) * LANES)
                rs = slice((u * pairs + p) * LANES, (u * pairs + p + 1) * LANES)
                o = _mm(pb[rs], v_scr[pl.ds(bases[u], A_BAND), cs]) * rinv[rs]
                o2 = jnp.where(lane < A_HEAD_DIM, o[:CHUNK], o[CHUNK:])
                o_ref[pl.ds(r0s[u], CHUNK), cs] = o2.astype(o_ref.dtype)
        return carry

    lax.fori_loop(0, tq // (CHUNK * nq), body, 0)


def _attention(qkv, bias_rows, layer, bsz, seq):
    tq = A_PAST_CHUNKS * CHUNK
    nblk = seq // tq
    cur = lambda c: (lambda b, i: (b * nblk + i, c))
    prev = lambda c: (lambda b, i: (b * nblk + jnp.maximum(i - 1, 0), c))
    blk = lambda f: pl.BlockSpec((tq, A_WIDTH), f)
    return pl.pallas_call(
        _attn_kernel,
        grid=(bsz, nblk),
        in_specs=[blk(cur(0)), blk(prev(1)), blk(cur(1)), blk(prev(2)), blk(cur(2)),
                  _resident((None, A_HEADS, BIAS_ROW), lambda b, i: (layer, 0, 0))],
        out_specs=blk(cur(0)),
        out_shape=jax.ShapeDtypeStruct((bsz * seq, A_WIDTH), BF16),
        scratch_shapes=[pltpu.VMEM((2 * tq, A_WIDTH), BF16), pltpu.VMEM((2 * tq, A_WIDTH), BF16),
                        pltpu.VMEM((A_HEADS * CHUNK, A_BAND), F32)],
        compiler_params=_params("arbitrary", "arbitrary"),
        name="chunk_band_attention",
    )(qkv, qkv, qkv, qkv, qkv, bias_rows)


def _lane_bcast(x, lane, width):
    return jnp.broadcast_to(x[:, lane:lane + 1], (x.shape[0], width))


def _gdn_kernel(x_ref, z_ref, ba_ref, hp_ref, ng_ref, o_ref,
                u_scr, w_scr, qd_scr, qk_scr, kdt_scr, gl_scr, o_scr, state):
    bsz, tb, _ = x_ref.shape
    nc = tb // CHUNK
    groups = bsz * nc
    rows = groups * CHUNK

    @pl.when(pl.program_id(0) == 0)
    def _():
        state[...] = jnp.zeros_like(state)

    ri = lax.broadcasted_iota(jnp.int32, (rows, CHUNK), 0) & (CHUNK - 1)
    ci = lax.broadcasted_iota(jnp.int32, (rows, CHUNK), 1)
    incl = ri >= ci
    strict = ri > ci
    r1 = lax.broadcasted_iota(jnp.int32, (CHUNK, CHUNK), 0)
    c1 = lax.broadcasted_iota(jnp.int32, (CHUNK, CHUNK), 1)
    tril = jnp.where(r1 >= c1, 1.0, 0.0).astype(F32)
    eye = jnp.where(ri == ci, 1.0, 0.0).astype(F32)

    ba = ba_ref[...].reshape(rows, LANES)
    beta_all = _sigmoid(ba)
    sp = ba + hp_ref[1:2, :]
    softplus = jnp.maximum(sp, 0.0) + jnp.log(1.0 + jnp.exp(-jnp.abs(sp)))
    g_all = -jnp.exp(hp_ref[0:1, :]) * softplus
    gc_all = jnp.concatenate(
        [jnp.dot(tril, _rows(g_all, g), preferred_element_type=F32, precision=lax.Precision.HIGHEST)
         for g in range(groups)], axis=0)
    gc_last = jnp.concatenate(
        [jnp.broadcast_to(gc_all[(g + 1) * CHUNK - 1:(g + 1) * CHUNK, :], (CHUNK, LANES))
         for g in range(groups)], axis=0)
    e_gc = jnp.exp(gc_all)
    e_rev = jnp.exp(gc_last - gc_all)
    e_last = jnp.exp(gc_last)

    for h in range(B_HEADS):
        q = x_ref[:, :, h * B_HEAD_DIM:(h + 1) * B_HEAD_DIM].reshape(rows, B_HEAD_DIM)
        k = x_ref[:, :, B_WIDTH + h * B_HEAD_DIM:B_WIDTH + (h + 1) * B_HEAD_DIM].reshape(rows, B_HEAD_DIM)
        v = x_ref[:, :, 2 * B_WIDTH + h * B_HEAD_DIM:2 * B_WIDTH + (h + 1) * B_HEAD_DIM].reshape(rows, B_HEAD_DIM)
        q = q * (lax.rsqrt(jnp.sum(q * q, axis=-1, keepdims=True) + EPS) * (B_HEAD_DIM ** -0.5))
        k = k * lax.rsqrt(jnp.sum(k * k, axis=-1, keepdims=True) + EPS)
        beta = _lane_bcast(beta_all, h, B_HEAD_DIM)
        egc = _lane_bcast(e_gc, B_HEADS + h, B_HEAD_DIM)
        gmat = _lane_bcast(g_all, B_HEADS + h, CHUNK)
        gc_col = _lane_bcast(gc_all, B_HEADS + h, CHUNK)
        gc_row = jnp.sum(jnp.where(ri <= ci, gmat, 0.0).reshape(groups, CHUNK, CHUNK), axis=1, keepdims=True)
        gc_row = jnp.broadcast_to(gc_row, (groups, CHUNK, CHUNK)).reshape(rows, CHUNK)
        decay = jnp.exp(jnp.where(incl, gc_col - gc_row, -jnp.inf))
        kb, qb = k.astype(BF16), q.astype(BF16)
        kk = jnp.concatenate([_mm_nt(_rows(kb, g), _rows(kb, g)) for g in range(groups)], axis=0)
        qk = jnp.concatenate([_mm_nt(_rows(qb, g), _rows(kb, g)) for g in range(groups)], axis=0)
        a = jnp.where(strict, beta[:, :CHUNK] * kk * decay, 0.0)
        t = eye - a
        p = a
        for _ in range(CHUNK.bit_length() - 2):
            p = _bmm(p, p, groups)
            t = t + _bmm(t, p, groups)
        rhs = jnp.concatenate([v * beta, k * (beta * egc)], axis=-1)
        sol = _bmm(t, rhs, groups)
        k_dec = k * _lane_bcast(e_rev, B_HEADS + h, B_HEAD_DIM)
        u_scr[h] = sol[:, :B_HEAD_DIM]
        w_scr[h] = sol[:, B_HEAD_DIM:].astype(BF16)
        qd_scr[h] = (q * egc).astype(BF16)
        qk_scr[h] = (qk * decay).astype(BF16)
        kdt_scr[h] = jnp.concatenate([_rows(k_dec, g).T for g in range(groups)], axis=0).astype(BF16)
        gl_scr[h] = _lane_bcast(e_last, B_HEADS + h, B_HEAD_DIM)

    chains = [(h, b) for h in range(B_HEADS) for b in range(bsz)]
    for c in range(nc):
        blk = lambda ref, h, b, n=CHUNK: ref[h, (b * nc + c) * n:(b * nc + c + 1) * n, :]
        s_old = [state[h * bsz + b] for h, b in chains]
        r1s = [_mm(jnp.concatenate([blk(w_scr, h, b), blk(qd_scr, h, b)], axis=0), s)
               for (h, b), s in zip(chains, s_old)]
        us = [blk(u_scr, h, b) - r[:CHUNK] for (h, b), r in zip(chains, r1s)]
        r2s = [_mm(jnp.concatenate([blk(qk_scr, h, b), blk(kdt_scr, h, b, B_HEAD_DIM)], axis=0), u)
               for (h, b), u in zip(chains, us)]
        for (h, b), s, ra, rb in zip(chains, s_old, r1s, r2s):
            gl = blk(gl_scr, h, b)
            state[h * bsz + b] = s * jnp.concatenate([gl, gl], axis=0) + rb[CHUNK:]
            o_scr[b, c * CHUNK:(c + 1) * CHUNK, h * B_HEAD_DIM:(h + 1) * B_HEAD_DIM] = ra[CHUNK:] + rb[:CHUNK]

    for h in range(B_HEADS):
        cs = slice(h * B_HEAD_DIM, (h + 1) * B_HEAD_DIM)
        o = o_scr[:, :, cs]
        o = o * lax.rsqrt(jnp.mean(o * o, axis=-1, keepdims=True) + EPS) * ng_ref[...] * _silu(z_ref[:, :, cs])
        o_ref[:, :, cs] = o.astype(o_ref.dtype)


def _gated_deltanet(qkv, z, ba, head_params, norm_g, layer):
    bsz, seq, width = qkv.shape
    tb = min(GDN_TILE, seq)
    rows = bsz * tb
    blk = lambda w: pl.BlockSpec((bsz, tb, w), lambda i: (0, i, 0))
    head_scr = lambda w, t: pltpu.VMEM((B_HEADS, rows, w), t)
    return pl.pallas_call(
        _gdn_kernel,
        grid=(seq // tb,),
        in_specs=[blk(width), blk(B_WIDTH), blk(LANES),
                  _resident((None, 2, LANES), lambda i: (layer, 0, 0)),
                  _resident((None, 1, B_HEAD_DIM), lambda i: (layer, 0, 0))],
        out_specs=blk(B_WIDTH),
        out_shape=jax.ShapeDtypeStruct((bsz, seq, B_WIDTH), BF16),
        scratch_shapes=[head_scr(B_HEAD_DIM, F32), head_scr(B_HEAD_DIM, BF16), head_scr(B_HEAD_DIM, BF16),
                        head_scr(CHUNK, BF16),
                        pltpu.VMEM((B_HEADS, rows * B_HEAD_DIM // CHUNK, CHUNK), BF16),
                        head_scr(B_HEAD_DIM, F32),
                        pltpu.VMEM((bsz, tb, B_WIDTH), F32),
                        pltpu.VMEM((B_HEADS * bsz, B_HEAD_DIM, B_HEAD_DIM), F32)],
        compiler_params=_params("arbitrary"),
        name="gated_deltanet",
    )(qkv, z, ba, head_params, norm_g)


def _merge_kernel(x_ref, ya_ref, yb_ref, g_ref, mod_ref, wa_ref, wb_ref, wo_ref, o_ref):
    d = x_ref.shape[-1]
    merged = (_sigmoid(g_ref[:, 0:d]) * jnp.dot(ya_ref[...], wa_ref[...], preferred_element_type=F32)
              + _sigmoid(g_ref[:, d:2 * d]) * jnp.dot(yb_ref[...], wb_ref[...], preferred_element_type=F32))
    o_ref[...] = x_ref[...] + mod_ref[2:3, :] * _mm(merged, wo_ref[...])


def _merge(x2, ya, yb, gates, mod_l, wa, wb, wo, layer, rows_per_batch):
    m, d = x2.shape
    tm = min(ROW_TILE, rows_per_batch)
    tpb = rows_per_batch // tm
    row_spec = lambda w: pl.BlockSpec((tm, w), lambda i: (i, 0))
    return pl.pallas_call(
        _merge_kernel,
        grid=(m // tm,),
        in_specs=[row_spec(d), row_spec(A_WIDTH), row_spec(B_WIDTH), row_spec(2 * d),
                  pl.BlockSpec((None, N_MOD, d), lambda i: (i // tpb, 0, 0)),
                  _resident((None, A_WIDTH, d), lambda i: (layer, 0, 0)),
                  _resident((None, B_WIDTH, d), lambda i: (layer, 0, 0)),
                  _resident((None, d, d), lambda i: (layer, 0, 0))],
        out_specs=row_spec(d),
        out_shape=jax.ShapeDtypeStruct((m, d), F32),
        compiler_params=_params("arbitrary"),
        name="merge_out_projection",
    )(x2, ya, yb, gates, mod_l, wa, wb, wo)


def _ffn_kernel(x_ref, mod_ref, g_ref, wi_ref, wo_ref, o_ref, acc_ref):
    x = x_ref[...]
    h = _norm_mod(x, g_ref[...], mod_ref[3:4, :], mod_ref[4:5, :]).astype(BF16)
    hidden = wo_ref.shape[0]
    for j in range(hidden // FFN_CHUNK):
        js = slice(j * FFN_CHUNK, (j + 1) * FFN_CHUNK)
        gate = jnp.dot(h, wi_ref[:, js], preferred_element_type=F32)
        up = jnp.dot(h, wi_ref[:, hidden + j * FFN_CHUNK:hidden + (j + 1) * FFN_CHUNK],
                     preferred_element_type=F32)
        act = (_silu(gate) * up).astype(BF16)
        part = jnp.dot(act, wo_ref[js, :], preferred_element_type=F32)
        if j == 0:
            acc_ref[...] = part
        else:
            acc_ref[...] += part
    o_ref[...] = x + mod_ref[5:6, :] * acc_ref[...]


def _ffn(x2, mod_l, gain, wi, wo, layer, rows_per_batch):
    m, d = x2.shape
    tm = min(ROW_TILE, rows_per_batch)
    tpb = rows_per_batch // tm
    hidden = wo.shape[1]
    row_spec = pl.BlockSpec((tm, d), lambda i: (i, 0))
    return pl.pallas_call(
        _ffn_kernel,
        grid=(m // tm,),
        in_specs=[row_spec,
                  pl.BlockSpec((None, N_MOD, d), lambda i: (i // tpb, 0, 0)),
                  _resident((None, 1, d), lambda i: (layer, 0, 0)),
                  _resident((None, d, 2 * hidden), lambda i: (layer, 0, 0)),
                  _resident((None, hidden, d), lambda i: (layer, 0, 0))],
        out_specs=row_spec,
        out_shape=jax.ShapeDtypeStruct((m, d), F32),
        scratch_shapes=[pltpu.VMEM((tm, d), F32)],
        compiler_params=_params("arbitrary"),
        name="swiglu_ffn",
    )(x2, mod_l, gain, wi, wo)


def _final_norm_kernel(x_ref, g_ref, o_ref):
    x = x_ref[...]
    o_ref[...] = x * lax.rsqrt(jnp.mean(x * x, axis=-1, keepdims=True) + EPS) * g_ref[...]


def _final_norm(x2, gain):
    m, d = x2.shape
    tm = min(ROW_TILE, m)
    return pl.pallas_call(
        _final_norm_kernel,
        grid=(m // tm,),
        in_specs=[pl.BlockSpec((tm, d), lambda i: (i, 0)), pl.BlockSpec((1, d), lambda i: (0, 0))],
        out_specs=pl.BlockSpec((tm, d), lambda i: (i, 0)),
        out_shape=jax.ShapeDtypeStruct((m, d), F32),
        compiler_params=_params("arbitrary"),
        name="final_rmsnorm",
    )(x2, gain)


def _bias_rows(rel_table):
    t = rel_table.astype(F32)
    lo = A_MAX_REL - (CHUNK - 1)
    ramp = t[..., lo:][..., ::-1]
    n_far = CHUNK + A_BAND - 1 - ramp.shape[-1]
    far = jnp.broadcast_to(t[..., -1:], t.shape[:-1] + (n_far,))
    pad = jnp.zeros(t.shape[:-1] + (BIAS_ROW - (CHUNK + A_BAND - 1),), F32)
    return jnp.concatenate([far, ramp, pad], axis=-1)


def kernel(x, c, w_ada, b_ada, norm1_g, norm2_g, w_in, rel_table, w_conv, a_log, dt_bias, gdn_norm_g,
           w_branch_a, w_branch_b, w_out, w_ffn_in, w_ffn_out, final_g):
    bsz, seq, d = x.shape
    depth = w_ada.shape[0]
    assert seq % (A_PAST_CHUNKS * CHUNK) == 0 and d % LANES == 0

    mod = _modulation(c, w_ada, b_ada)
    n_main = 3 * A_WIDTH + 4 * B_WIDTH
    n_ba = 2 * B_HEADS
    w_main = w_in[..., :n_main].astype(BF16)
    w_ba = jnp.pad(w_in[..., n_main:n_main + n_ba], ((0, 0), (0, 0), (0, LANES - n_ba))).astype(BF16)
    w_gates = w_in[..., n_main + n_ba:].astype(BF16)
    wi, wo_ffn = w_ffn_in.astype(BF16), w_ffn_out.astype(BF16)
    wa, wb, wo = w_branch_a.astype(BF16), w_branch_b.astype(BF16), w_out.astype(BF16)
    bias_rows = _bias_rows(rel_table)
    lane_pad = jnp.zeros((depth, LANES - 2 * B_HEADS), F32)
    zeros_h = jnp.zeros((depth, B_HEADS), F32)
    head_params = jnp.stack([jnp.concatenate([zeros_h, a_log, lane_pad], axis=-1),
                             jnp.concatenate([zeros_h, dt_bias, lane_pad], axis=-1)], axis=1)
    g1 = norm1_g.reshape(depth, 1, d)
    g2 = norm2_g.reshape(depth, 1, d)
    ng = gdn_norm_g.reshape(depth, 1, B_HEAD_DIM)

    x2 = x.reshape(bsz * seq, d)
    for l in range(depth):
        qkv_a, qkv_b, z_b, gates, ba = _in_projection(x2, mod[l], g1, w_main, w_gates, w_ba, w_conv, l, seq)
        ya = _attention(qkv_a, bias_rows, l, bsz, seq)
        yb = _gated_deltanet(qkv_b.reshape(bsz, seq, -1), z_b.reshape(bsz, seq, -1),
                             ba.reshape(bsz, seq, -1), head_params, ng, l)
        x2 = _merge(x2, ya, yb.reshape(bsz * seq, -1), gates, mod[l], wa, wb, wo, l, seq)
        x2 = _ffn(x2, mod[l], g2, wi, wo_ffn, l, seq)
    return _final_norm(x2, final_g.reshape(1, d)).reshape(bsz, seq, d)
```

```python
import functools

import jax
import jax.numpy as jnp
from jax import lax
from jax.experimental import pallas as pl
from jax.experimental.pallas import tpu as pltpu

F32 = jnp.float32
BF16 = jnp.bfloat16

EPS = 1e-6
CHUNK = 64
A_HEADS = 8
A_HEAD_DIM = 64
A_WIDTH = A_HEADS * A_HEAD_DIM
A_PAST_CHUNKS = 8
A_BAND = (A_PAST_CHUNKS + 1) * CHUNK
A_MAX_REL = 128
B_HEADS = 4
B_HEAD_DIM = 128
B_WIDTH = B_HEADS * B_HEAD_DIM
CONV_K = 4
N_MOD = 6
LANES = 128
SUBLANES = 8
VMEM_LIMIT = 56 * 1024 * 1024

ROW_TILE = 512
COL_TILE = 256
FFN_CHUNK = 256
GDN_TILE = 128
ATTN_CHUNKS_PER_ITER = 2
BIAS_ROW = 5 * LANES


def _params(*sem):
    return pltpu.CompilerParams(dimension_semantics=sem, vmem_limit_bytes=VMEM_LIMIT)


def _resident(block_shape, index_map):
    return pl.BlockSpec(block_shape, index_map, pipeline_mode=pl.Buffered(1))


def _mm(a, b):
    return jnp.dot(a.astype(BF16), b.astype(BF16), preferred_element_type=F32)


def _mm_nt(a, b):
    return lax.dot_general(a.astype(BF16), b.astype(BF16), (((1,), (1,)), ((), ())),
                           preferred_element_type=F32)


def _rows(x, g, n=CHUNK):
    return x[g * n:(g + 1) * n]


def _bmm(a, b, groups, rows_a=CHUNK, rows_b=CHUNK):
    return jnp.concatenate([_mm(_rows(a, g, rows_a), _rows(b, g, rows_b)) for g in range(groups)], axis=0)


def _sigmoid(x):
    return 1.0 / (1.0 + jnp.exp(-x))


def _silu(x):
    return x * _sigmoid(x)


def _norm_mod(x, gain, shift, scale):
    ms = jnp.mean(x * x, axis=-1, keepdims=True)
    return x * lax.rsqrt(ms + EPS) * gain * (1.0 + scale) + shift


def _mod_kernel(c_ref, w_ref, b_ref, o_ref):
    cond = _silu(c_ref[...])
    o_ref[...] = _mm(cond, w_ref[...]) + b_ref[...]


def _modulation(c, w_ada, b_ada):
    depth, d, n = w_ada.shape
    bsz = c.shape[0]
    rows = -(-bsz // SUBLANES) * SUBLANES
    c_pad = jnp.zeros((rows, d), F32).at[:bsz].set(c)
    tn = n // 4
    out = pl.pallas_call(
        _mod_kernel,
        grid=(depth, n // tn),
        in_specs=[
            pl.BlockSpec((rows, d), lambda l, j: (0, 0)),
            pl.BlockSpec((None, d, tn), lambda l, j: (l, 0, j)),
            pl.BlockSpec((None, 1, tn), lambda l, j: (l, 0, j)),
        ],
        out_specs=pl.BlockSpec((None, rows, tn), lambda l, j: (l, 0, j)),
        out_shape=jax.ShapeDtypeStruct((depth, rows, n), F32),
        compiler_params=_params("arbitrary", "arbitrary"),
        name="adaln_modulation",
    )(c_pad, w_ada, b_ada.reshape(depth, 1, n))
    return out[:, :bsz].reshape(depth, bsz, N_MOD, d)


def _inproj_kernel(x_ref, mod_ref, g_ref, wm_ref, wg_ref, wc_ref, brow_ref,
                   ya_ref, ob_ref, oz_ref, og_ref, oba_ref, xext, q_scr, k_scr, v_scr, bias_scr,
                   *, tiles_per_batch):
    tm = x_ref.shape[0]
    a3, b3 = 3 * A_WIDTH, ob_ref.shape[1]
    first = pl.program_id(0) % tiles_per_batch == 0
    pairs = A_WIDTH // LANES
    heads_per_pair = LANES // A_HEAD_DIM
    past = A_PAST_CHUNKS * CHUNK
    nq = ATTN_CHUNKS_PER_ITER

    @pl.when(pl.program_id(0) == 0)
    def _():
        for hd in range(A_HEADS):
            tiled = jnp.broadcast_to(brow_ref[hd:hd + 1, :], (CHUNK, BIAS_ROW))
            rolled = pltpu.roll(tiled, BIAS_ROW - (CHUNK - 1), 1, stride=1, stride_axis=0)
            bias_scr[hd * CHUNK:(hd + 1) * CHUNK, :] = rolled[:, :A_BAND]
        k_scr[...] = jnp.zeros_like(k_scr)
        v_scr[...] = jnp.zeros_like(v_scr)

    @pl.when(first)
    def _():
        xext[0:SUBLANES, :] = jnp.zeros((SUBLANES, b3), F32)

    k_scr[0:tm, :] = k_scr[tm:2 * tm, :]
    v_scr[0:tm, :] = v_scr[tm:2 * tm, :]

    h = _norm_mod(x_ref[...], g_ref[...], mod_ref[0:1, :], mod_ref[1:2, :]).astype(BF16)

    def proj(w_ref, c0, cw):
        return jnp.dot(h, w_ref[:, c0:c0 + cw], preferred_element_type=F32)

    scale = A_HEAD_DIM ** -0.5
    for c0 in range(0, A_WIDTH, COL_TILE):
        q_scr[:, c0:c0 + COL_TILE] = proj(wm_ref, c0, COL_TILE).astype(BF16) * scale
        k_scr[tm:2 * tm, c0:c0 + COL_TILE] = proj(wm_ref, A_WIDTH + c0, COL_TILE).astype(BF16)
        v_scr[tm:2 * tm, c0:c0 + COL_TILE] = proj(wm_ref, 2 * A_WIDTH + c0, COL_TILE).astype(BF16)

    lane = lax.broadcasted_iota(jnp.int32, (CHUNK, LANES), 1)
    col = lax.broadcasted_iota(jnp.int32, (1, A_BAND), 1)

    def attn_scores(it):
        slabs = []
        for u in range(nq):
            r0 = (it * nq + u) * CHUNK
            base = r0 + (tm - past)
            parts = []
            for p in range(pairs):
                cs = slice(p * LANES, (p + 1) * LANES)
                q2 = q_scr[r0:r0 + CHUNK, cs]
                zero = jnp.zeros_like(q2)
                qs = jnp.concatenate(
                    [jnp.where((lane >= hh * A_HEAD_DIM) & (lane < (hh + 1) * A_HEAD_DIM), q2, zero)
                     for hh in range(heads_per_pair)], axis=0)
                parts.append(_mm_nt(qs, k_scr[base:base + A_BAND, cs]))
            valid = jnp.logical_or(col + base >= tm, jnp.logical_not(first))
            slabs.append(jnp.concatenate(parts, axis=0) + bias_scr[...] + jnp.where(valid, 0.0, -1e30))
        return jnp.concatenate(slabs, axis=0)

    def attn_softmax(s):
        e = jnp.exp(s - jnp.max(s, axis=-1, keepdims=True))
        return e.astype(BF16), 1.0 / jnp.sum(e, axis=-1, keepdims=True)

    def attn_out(it, pb, rinv):
        for u in range(nq):
            r0 = (it * nq + u) * CHUNK
            base = r0 + (tm - past)
            for p in range(pairs):
                cs = slice(p * LANES, (p + 1) * LANES)
                rs = slice((u * pairs + p) * LANES, (u * pairs + p + 1) * LANES)
                o = _mm(pb[rs], v_scr[base:base + A_BAND, cs]) * rinv[rs]
                o2 = jnp.where(lane < A_HEAD_DIM, o[:CHUNK], o[CHUNK:])
                ya_ref[r0:r0 + CHUNK, cs] = o2.astype(ya_ref.dtype)

    def conv_tile(c0):
        full = xext[:, c0:c0 + LANES]
        acc = wc_ref[CONV_K - 1:CONV_K, c0:c0 + LANES] * full[SUBLANES:]
        for s in range(1, CONV_K):
            shifted = pltpu.roll(full, s, 0)[SUBLANES:]
            acc = acc + wc_ref[CONV_K - 1 - s:CONV_K - s, c0:c0 + LANES] * shifted
        ob_ref[:, c0:c0 + LANES] = _silu(acc)

    def raw_job(r0):
        xext[SUBLANES:SUBLANES + tm, r0:r0 + COL_TILE] = proj(wm_ref, a3 + r0, COL_TILE)
        for c0 in range(r0, r0 + COL_TILE, LANES):
            conv_tile(c0)

    def plain_job(o_ref, w_ref, wc0, oc0):
        cw = min(COL_TILE, o_ref.shape[1])
        o_ref[:, oc0:oc0 + cw] = proj(w_ref, wc0, cw).astype(o_ref.dtype)

    plain = [(og_ref, wg_ref, c0, c0) for c0 in range(0, og_ref.shape[1], COL_TILE)]
    plain += [(oz_ref, wm_ref, a3 + b3 + c0, c0) for c0 in range(0, oz_ref.shape[1], COL_TILE)]
    plain += [(oba_ref, wm_ref, a3 + b3 + oz_ref.shape[1], 0)]
    raw_tiles = list(range(0, b3, COL_TILE))
    jobs = []
    for n, r0 in enumerate(raw_tiles):
        jobs.append(functools.partial(raw_job, r0))
        lo, hi = (len(plain) * k // len(raw_tiles) for k in (n, n + 1))
        jobs += [functools.partial(plain_job, *spec) for spec in plain[lo:hi]]
    n_iter = tm // (CHUNK * nq)
    n_phase = 3 * n_iter
    state = {}
    for j, job in enumerate(jobs):
        for ph in range(n_phase):
            if ph * len(jobs) // n_phase != j:
                continue
            it, kind = divmod(ph, 3)
            if kind == 0:
                state["s"] = attn_scores(it)
            elif kind == 1:
                state["p"] = attn_softmax(state.pop("s"))
            else:
                attn_out(it, *state.pop("p"))
        job()
    xext[0:SUBLANES, :] = xext[tm:tm + SUBLANES, :]


def _in_projection(x2, mod_l, gain, w_all, w_gates, w_conv, bias_rows, layer, rows_per_batch):
    m, d = x2.shape
    tm = A_PAST_CHUNKS * CHUNK
    tpb = rows_per_batch // tm
    n_window = 3 * A_WIDTH + 4 * B_WIDTH + LANES
    widths = (A_WIDTH, 3 * B_WIDTH, B_WIDTH, 2 * d, LANES)
    dtypes = (BF16, F32, F32, F32, F32)
    row_spec = lambda w: pl.BlockSpec((tm, w), lambda i: (i, 0))
    return pl.pallas_call(
        functools.partial(_inproj_kernel, tiles_per_batch=tpb),
        grid=(m // tm,),
        in_specs=[
            row_spec(d),
            pl.BlockSpec((None, N_MOD, d), lambda i: (i // tpb, 0, 0)),
            _resident((None, 1, d), lambda i: (layer, 0, 0)),
            _resident((None, d, n_window), lambda i: (layer, 0, 0)),
            _resident((None, d, 2 * d), lambda i: (layer, 0, 0)),
            _resident((None, CONV_K, 3 * B_WIDTH), lambda i: (layer, 0, 0)),
            _resident((None, A_HEADS, BIAS_ROW), lambda i: (layer, 0, 0)),
        ],
        out_specs=[row_spec(w) for w in widths],
        out_shape=[jax.ShapeDtypeStruct((m, w), t) for w, t in zip(widths, dtypes)],
        scratch_shapes=[pltpu.VMEM((tm + SUBLANES, 3 * B_WIDTH), F32),
                        pltpu.VMEM((tm, A_WIDTH), BF16),
                        pltpu.VMEM((2 * tm, A_WIDTH), BF16), pltpu.VMEM((2 * tm, A_WIDTH), BF16),
                        pltpu.VMEM((A_HEADS * CHUNK, A_BAND), F32)],
        compiler_params=_params("arbitrary"),
        name="in_projection_attention",
    )(x2, mod_l, gain, w_all, w_gates, w_conv, bias_rows)


def _lane_bcast(x, lane, width):
    return jnp.broadcast_to(x[:, lane:lane + 1], (x.shape[0], width))


def _gdn_kernel(x_ref, z_ref, ba_ref, hp_ref, ng_ref, o_ref,
                u_scr, w_scr, qd_scr, qk_scr, kdt_scr, gl_scr, o_scr, state):
    bsz, tb, _ = x_ref.shape
    nc = tb // CHUNK
    groups = bsz * nc
    rows = groups * CHUNK

    @pl.when(pl.program_id(0) == 0)
    def _():
        state[...] = jnp.zeros_like(state)

    ri = lax.broadcasted_iota(jnp.int32, (rows, CHUNK), 0) & (CHUNK - 1)
    ci = lax.broadcasted_iota(jnp.int32, (rows, CHUNK), 1)
    incl = ri >= ci
    strict = ri > ci
    r1 = lax.broadcasted_iota(jnp.int32, (CHUNK, CHUNK), 0)
    c1 = lax.broadcasted_iota(jnp.int32, (CHUNK, CHUNK), 1)
    tril = jnp.where(r1 >= c1, 1.0, 0.0).astype(F32)
    eye = jnp.where(ri == ci, 1.0, 0.0).astype(F32)

    def pair_mask(b):
        return ((ri ^ ci) < 2 * b) & ((ri & b) != 0) & ((ci & b) == 0)

    ba = ba_ref[...].reshape(rows, LANES)
    beta_all = _sigmoid(ba)
    sp = ba + hp_ref[1:2, :]
    softplus = jnp.maximum(sp, 0.0) + jnp.log(1.0 + jnp.exp(-jnp.abs(sp)))
    g_all = -jnp.exp(hp_ref[0:1, :]) * softplus
    gc_all = jnp.concatenate(
        [jnp.dot(tril, _rows(g_all, g), preferred_element_type=F32, precision=lax.Precision.HIGHEST)
         for g in range(groups)], axis=0)
    gc_last = jnp.concatenate(
        [jnp.broadcast_to(gc_all[(g + 1) * CHUNK - 1:(g + 1) * CHUNK, :], (CHUNK, LANES))
         for g in range(groups)], axis=0)
    e_gc = jnp.exp(gc_all)
    e_rev = jnp.exp(gc_last - gc_all)
    e_last = jnp.exp(gc_last)

    heads = range(B_HEADS)
    ts, ps, rhss = [], [], []
    for h in heads:
        q = x_ref[:, :, h * B_HEAD_DIM:(h + 1) * B_HEAD_DIM].reshape(rows, B_HEAD_DIM)
        k = x_ref[:, :, B_WIDTH + h * B_HEAD_DIM:B_WIDTH + (h + 1) * B_HEAD_DIM].reshape(rows, B_HEAD_DIM)
        v = x_ref[:, :, 2 * B_WIDTH + h * B_HEAD_DIM:2 * B_WIDTH + (h + 1) * B_HEAD_DIM].reshape(rows, B_HEAD_DIM)
        q = q * (lax.rsqrt(jnp.sum(q * q, axis=-1, keepdims=True) + EPS) * (B_HEAD_DIM ** -0.5))
        k = k * lax.rsqrt(jnp.sum(k * k, axis=-1, keepdims=True) + EPS)
        beta = _lane_bcast(beta_all, h, B_HEAD_DIM)
        egc = _lane_bcast(e_gc, B_HEADS + h, B_HEAD_DIM)
        gmat = _lane_bcast(g_all, B_HEADS + h, CHUNK)
        gc_col = _lane_bcast(gc_all, B_HEADS + h, CHUNK)
        gc_row = jnp.sum(jnp.where(ri <= ci, gmat, 0.0).reshape(groups, CHUNK, CHUNK), axis=1, keepdims=True)
        gc_row = jnp.broadcast_to(gc_row, (groups, CHUNK, CHUNK)).reshape(rows, CHUNK)
        decay = jnp.exp(jnp.where(incl, gc_col - gc_row, -jnp.inf))
        kb, qb = k.astype(BF16), q.astype(BF16)
        kk = jnp.concatenate([_mm_nt(_rows(kb, g), _rows(kb, g)) for g in range(groups)], axis=0)
        qk = jnp.concatenate([_mm_nt(_rows(qb, g), _rows(kb, g)) for g in range(groups)], axis=0)
        a = jnp.where(strict, beta[:, :CHUNK] * kk * decay, 0.0)
        ts.append(eye - jnp.where(pair_mask(1), a, 0.0))
        ps.append(a)
        rhss.append(jnp.concatenate([v * beta, k * (beta * egc)], axis=-1))
        k_dec = k * _lane_bcast(e_rev, B_HEADS + h, B_HEAD_DIM)
        qd_scr[h] = (q * egc).astype(BF16)
        qk_scr[h] = (qk * decay).astype(BF16)
        kdt_scr[h] = jnp.concatenate([_rows(k_dec, g).T for g in range(groups)], axis=0).astype(BF16)
        gl_scr[h] = _lane_bcast(e_last, B_HEADS + h, B_HEAD_DIM)
    b = 2
    while b < CHUNK:
        ns = [jnp.where(pair_mask(b), a, 0.0) for a in ps]
        ws = [_bmm(n, t, groups) for n, t in zip(ns, ts)]
        ts = [t - _bmm(t, w, groups) for t, w in zip(ts, ws)]
        b *= 2
    for h in heads:
        sol = _bmm(ts[h], rhss[h], groups)
        u_scr[h] = sol[:, :B_HEAD_DIM]
        w_scr[h] = sol[:, B_HEAD_DIM:].astype(BF16)

    chains = [(h, b) for h in range(B_HEADS) for b in range(bsz)]
    for c in range(nc):
        blk = lambda ref, h, b, n=CHUNK: ref[h, (b * nc + c) * n:(b * nc + c + 1) * n, :]
        s_old = [state[h * bsz + b] for h, b in chains]
        r1s = [_mm(jnp.concatenate([blk(w_scr, h, b), blk(qd_scr, h, b)], axis=0), s)
               for (h, b), s in zip(chains, s_old)]
        us = [blk(u_scr, h, b) - r[:CHUNK] for (h, b), r in zip(chains, r1s)]
        r2s = [_mm(jnp.concatenate([blk(qk_scr, h, b), blk(kdt_scr, h, b, B_HEAD_DIM)], axis=0), u)
               for (h, b), u in zip(chains, us)]
        for (h, b), s, ra, rb in zip(chains, s_old, r1s, r2s):
            gl = blk(gl_scr, h, b)
            state[h * bsz + b] = s * jnp.concatenate([gl, gl], axis=0) + rb[CHUNK:]
            o_scr[b, c * CHUNK:(c + 1) * CHUNK, h * B_HEAD_DIM:(h + 1) * B_HEAD_DIM] = ra[CHUNK:] + rb[:CHUNK]

    for h in range(B_HEADS):
        cs = slice(h * B_HEAD_DIM, (h + 1) * B_HEAD_DIM)
        o = o_scr[:, :, cs]
        o = o * lax.rsqrt(jnp.mean(o * o, axis=-1, keepdims=True) + EPS) * ng_ref[...] * _silu(z_ref[:, :, cs])
        o_ref[:, :, cs] = o.astype(o_ref.dtype)


def _gated_deltanet(qkv, z, ba, head_params, norm_g, layer):
    bsz, seq, width = qkv.shape
    tb = min(GDN_TILE, seq)
    rows = bsz * tb
    blk = lambda w: pl.BlockSpec((bsz, tb, w), lambda i: (0, i, 0))
    head_scr = lambda w, t: pltpu.VMEM((B_HEADS, rows, w), t)
    return pl.pallas_call(
        _gdn_kernel,
        grid=(seq // tb,),
        in_specs=[blk(width), blk(B_WIDTH), blk(LANES),
                  _resident((None, 2, LANES), lambda i: (layer, 0, 0)),
                  _resident((None, 1, B_HEAD_DIM), lambda i: (layer, 0, 0))],
        out_specs=blk(B_WIDTH),
        out_shape=jax.ShapeDtypeStruct((bsz, seq, B_WIDTH), BF16),
        scratch_shapes=[head_scr(B_HEAD_DIM, F32), head_scr(B_HEAD_DIM, BF16), head_scr(B_HEAD_DIM, BF16),
                        head_scr(CHUNK, BF16),
                        pltpu.VMEM((B_HEADS, rows * B_HEAD_DIM // CHUNK, CHUNK), BF16),
                        head_scr(B_HEAD_DIM, F32),
                        pltpu.VMEM((bsz, tb, B_WIDTH), F32),
                        pltpu.VMEM((B_HEADS * bsz, B_HEAD_DIM, B_HEAD_DIM), F32)],
        compiler_params=_params("arbitrary"),
        name="gated_deltanet",
    )(qkv, z, ba, head_params, norm_g)


def _mix_ffn_kernel(x_ref, ya_ref, yb_ref, gt_ref, mod_ref, g_ref, fg_ref, wa_ref, wb_ref, wm_ref, wi_ref, wo_ref,
                    o_ref, x1_ref, acc_ref, *, final):
    d = x_ref.shape[-1]
    merged = (_sigmoid(gt_ref[:, 0:d]) * jnp.dot(ya_ref[...], wa_ref[...], preferred_element_type=F32)
              + _sigmoid(gt_ref[:, d:2 * d]) * jnp.dot(yb_ref[...], wb_ref[...], preferred_element_type=F32))
    x1_ref[...] = x_ref[...] + mod_ref[2:3, :] * _mm(merged, wm_ref[...])
    h = _norm_mod(x1_ref[...], g_ref[...], mod_ref[3:4, :], mod_ref[4:5, :]).astype(BF16)
    hidden = wo_ref.shape[0]
    for j in range(hidden // FFN_CHUNK):
        js = slice(j * FFN_CHUNK, (j + 1) * FFN_CHUNK)
        gate = jnp.dot(h, wi_ref[:, js], preferred_element_type=F32)
        up = jnp.dot(h, wi_ref[:, hidden + j * FFN_CHUNK:hidden + (j + 1) * FFN_CHUNK],
                     preferred_element_type=F32)
        act = (_silu(gate) * up).astype(BF16)
        part = jnp.dot(act, wo_ref[js, :], preferred_element_type=F32)
        if j == 0:
            acc_ref[...] = part
        else:
            acc_ref[...] += part
    out = x1_ref[...] + mod_ref[5:6, :] * acc_ref[...]
    if final:
        out = out * lax.rsqrt(jnp.mean(out * out, axis=-1, keepdims=True) + EPS) * fg_ref[...]
    o_ref[...] = out


def _mix_ffn(x2, ya, yb, gates, mod_l, gain, final_g, wa, wb, wm, wi, wo, layer, rows_per_batch, final):
    m, d = x2.shape
    tm = min(ROW_TILE, rows_per_batch)
    tpb = rows_per_batch // tm
    hidden = wo.shape[1]
    row_spec = lambda w: pl.BlockSpec((tm, w), lambda i: (i, 0))
    return pl.pallas_call(
        functools.partial(_mix_ffn_kernel, final=final),
        grid=(m // tm,),
        in_specs=[row_spec(d), row_spec(A_WIDTH), row_spec(B_WIDTH), row_spec(2 * d),
                  pl.BlockSpec((None, N_MOD, d), lambda i: (i // tpb, 0, 0)),
                  _resident((None, 1, d), lambda i: (layer, 0, 0)),
                  _resident((1, d), lambda i: (0, 0)),
                  _resident((None, A_WIDTH, d), lambda i: (layer, 0, 0)),
                  _resident((None, B_WIDTH, d), lambda i: (layer, 0, 0)),
                  _resident((None, d, d), lambda i: (layer, 0, 0)),
                  _resident((None, d, 2 * hidden), lambda i: (layer, 0, 0)),
                  _resident((None, hidden, d), lambda i: (layer, 0, 0))],
        out_specs=row_spec(d),
        out_shape=jax.ShapeDtypeStruct((m, d), F32),
        scratch_shapes=[pltpu.VMEM((tm, d), F32), pltpu.VMEM((tm, d), F32)],
        compiler_params=_params("arbitrary"),
        name="merge_swiglu",
    )(x2, ya, yb, gates, mod_l, gain, final_g, wa, wb, wm, wi, wo)


def _bias_rows(rel_table):
    t = rel_table.astype(F32)
    lo = A_MAX_REL - (CHUNK - 1)
    ramp = t[..., lo:][..., ::-1]
    n_far = CHUNK + A_BAND - 1 - ramp.shape[-1]
    far = jnp.broadcast_to(t[..., -1:], t.shape[:-1] + (n_far,))
    pad = jnp.zeros(t.shape[:-1] + (BIAS_ROW - (CHUNK + A_BAND - 1),), F32)
    return jnp.concatenate([far, ramp, pad], axis=-1)


def kernel(x, c, w_ada, b_ada, norm1_g, norm2_g, w_in, rel_table, w_conv, a_log, dt_bias, gdn_norm_g,
           w_branch_a, w_branch_b, w_out, w_ffn_in, w_ffn_out, final_g):
    bsz, seq, d = x.shape
    depth = w_ada.shape[0]
    assert seq % (A_PAST_CHUNKS * CHUNK) == 0 and d % LANES == 0

    mod = _modulation(c, w_ada, b_ada)
    w_all = w_in.astype(BF16)
    w_gates = w_in[..., 3 * A_WIDTH + 4 * B_WIDTH + 2 * B_HEADS:].astype(BF16)
    wi, wo_ffn = w_ffn_in.astype(BF16), w_ffn_out.astype(BF16)
    wa, wb, wo = w_branch_a.astype(BF16), w_branch_b.astype(BF16), w_out.astype(BF16)
    bias_rows = _bias_rows(rel_table)
    lane_pad = jnp.zeros((depth, LANES - 2 * B_HEADS), F32)
    zeros_h = jnp.zeros((depth, B_HEADS), F32)
    head_params = jnp.stack([jnp.concatenate([zeros_h, a_log, lane_pad], axis=-1),
                             jnp.concatenate([zeros_h, dt_bias, lane_pad], axis=-1)], axis=1)
    g1 = norm1_g.reshape(depth, 1, d)
    g2 = norm2_g.reshape(depth, 1, d)
    ng = gdn_norm_g.reshape(depth, 1, B_HEAD_DIM)

    x2 = x.reshape(bsz * seq, d)
    for l in range(depth):
        ya, qkv_b, z_b, gates, ba = _in_projection(x2, mod[l], g1, w_all, w_gates, w_conv, bias_rows, l, seq)
        yb = _gated_deltanet(qkv_b.reshape(bsz, seq, -1), z_b.reshape(bsz, seq, -1),
                             ba.reshape(bsz, seq, -1), head_params, ng, l)
        x2 = _mix_ffn(x2, ya, yb.reshape(bsz * seq, -1), gates, mod[l], g2, final_g.reshape(1, d),
                      wa, wb, wo, wi, wo_ffn, l, seq, final=(l == depth - 1))
    return x2.reshape(bsz, seq, d)
```

```python
import functools

import jax
import jax.numpy as jnp
from jax import lax
from jax.experimental import pallas as pl
from jax.experimental.pallas import tpu as pltpu

F32 = jnp.float32
BF16 = jnp.bfloat16

EPS = 1e-6
CHUNK = 64
A_HEADS = 8
A_HEAD_DIM = 64
A_WIDTH = A_HEADS * A_HEAD_DIM
A_PAST_CHUNKS = 8
A_BAND = (A_PAST_CHUNKS + 1) * CHUNK
A_MAX_REL = 128
B_HEADS = 4
B_HEAD_DIM = 128
B_WIDTH = B_HEADS * B_HEAD_DIM
CONV_K = 4
N_MOD = 6
LANES = 128
SUBLANES = 8
VMEM_LIMIT = 60 * 1024 * 1024

ROW_TILE = A_PAST_CHUNKS * CHUNK
COL_TILE = 256
FFN_CHUNK = 256
ATTN_PAIRS_PER_PHASE = 4
BIAS_ROW = 5 * LANES


def _params(*sem):
    return pltpu.CompilerParams(dimension_semantics=sem, vmem_limit_bytes=VMEM_LIMIT)


def _resident(block_shape, index_map):
    return pl.BlockSpec(block_shape, index_map, pipeline_mode=pl.Buffered(1))


def _mm(a, b):
    return jnp.dot(a.astype(BF16), b.astype(BF16), preferred_element_type=F32)


def _mm_nt(a, b):
    return lax.dot_general(a.astype(BF16), b.astype(BF16), (((1,), (1,)), ((), ())),
                           preferred_element_type=F32)


def _rows(x, g, n=CHUNK):
    return x[g * n:(g + 1) * n]


def _bmm(a, b, groups, rows_a=CHUNK, rows_b=CHUNK):
    return jnp.concatenate([_mm(_rows(a, g, rows_a), _rows(b, g, rows_b)) for g in range(groups)], axis=0)


def _sigmoid(x):
    return 1.0 / (1.0 + jnp.exp(-x))


def _silu(x):
    return x * _sigmoid(x)


def _norm_mod(x, gain, shift, scale):
    ms = jnp.mean(x * x, axis=-1, keepdims=True)
    return x * lax.rsqrt(ms + EPS) * gain * (1.0 + scale) + shift


def _lane_bcast(x, lane, width):
    return jnp.broadcast_to(x[:, lane:lane + 1], (x.shape[0], width))


def _interleave(*lists):
    tagged = sorted(((i + 0.5) / len(l), n, i) for n, l in enumerate(lists) for i in range(len(l)))
    return [lists[n][i] for _, n, i in tagged]


def _mod_kernel(c_ref, w_ref, b_ref, o_ref):
    cond = _silu(c_ref[...])
    o_ref[...] = _mm(cond, w_ref[...]) + b_ref[...]


def _modulation(c, w_ada, b_ada):
    depth, d, n = w_ada.shape
    bsz = c.shape[0]
    rows = -(-bsz // SUBLANES) * SUBLANES
    c_pad = jnp.zeros((rows, d), F32).at[:bsz].set(c)
    tn = n // 4
    out = pl.pallas_call(
        _mod_kernel,
        grid=(depth, n // tn),
        in_specs=[
            pl.BlockSpec((rows, d), lambda l, j: (0, 0)),
            pl.BlockSpec((None, d, tn), lambda l, j: (l, 0, j)),
            pl.BlockSpec((None, 1, tn), lambda l, j: (l, 0, j)),
        ],
        out_specs=pl.BlockSpec((None, rows, tn), lambda l, j: (l, 0, j)),
        out_shape=jax.ShapeDtypeStruct((depth, rows, n), F32),
        compiler_params=_params("arbitrary", "arbitrary"),
        name="adaln_modulation",
    )(c_pad, w_ada, b_ada.reshape(depth, 1, n))
    return out[:, :bsz].reshape(depth, bsz, N_MOD, d)


def _attention_jobs(q_scr, k_scr, v_scr, bias_scr, ya_ref, first):
    tm = q_scr.shape[0]
    pairs = A_WIDTH // LANES
    heads_per_pair = LANES // A_HEAD_DIM
    past = A_PAST_CHUNKS * CHUNK
    lane = lax.broadcasted_iota(jnp.int32, (CHUNK, LANES), 1)
    col = lax.broadcasted_iota(jnp.int32, (1, A_BAND), 1)
    live = {}

    def scores(r0, p0):
        base = r0 + (tm - past)
        parts = []
        for p in range(p0, p0 + ATTN_PAIRS_PER_PHASE):
            cs = slice(p * LANES, (p + 1) * LANES)
            q2 = q_scr[r0:r0 + CHUNK, cs]
            zero = jnp.zeros_like(q2)
            qs = jnp.concatenate(
                [jnp.where((lane >= hh * A_HEAD_DIM) & (lane < (hh + 1) * A_HEAD_DIM), q2, zero)
                 for hh in range(heads_per_pair)], axis=0)
            parts.append(_mm_nt(qs, k_scr[base:base + A_BAND, cs]))
        valid = jnp.logical_or(col + base >= tm, jnp.logical_not(first))
        bias = bias_scr[p0 * LANES:(p0 + ATTN_PAIRS_PER_PHASE) * LANES, :]
        live["s"] = jnp.concatenate(parts, axis=0) + bias + jnp.where(valid, 0.0, -1e30)

    def softmax(r0, p0):
        s = live.pop("s")
        e = jnp.exp(s - jnp.max(s, axis=-1, keepdims=True))
        live["p"] = (e.astype(BF16), 1.0 / jnp.sum(e, axis=-1, keepdims=True))

    def output(r0, p0):
        pb, rinv = live.pop("p")
        base = r0 + (tm - past)
        for n, p in enumerate(range(p0, p0 + ATTN_PAIRS_PER_PHASE)):
            cs = slice(p * LANES, (p + 1) * LANES)
            rs = slice(n * LANES, (n + 1) * LANES)
            o = _mm(pb[rs], v_scr[base:base + A_BAND, cs]) * rinv[rs]
            o2 = jnp.where(lane < A_HEAD_DIM, o[:CHUNK], o[CHUNK:])
            ya_ref[r0:r0 + CHUNK, cs] = o2.astype(ya_ref.dtype)

    return [functools.partial(f, r0, p0) for r0 in range(0, tm, CHUNK)
            for p0 in range(0, pairs, ATTN_PAIRS_PER_PHASE) for f in (scores, softmax, output)]


def _gdn_jobs(x_ref, z_ref, ba_ref, hp_ref, ng_ref, o_ref,
              u_scr, w_scr, qd_scr, qk_scr, kdt_scr, gl_scr, o_scr, state):
    rows = x_ref.shape[0]
    groups = rows // CHUNK
    heads = range(B_HEADS)
    v_ = {}

    def pair_mask(b):
        ri, ci = v_["ri"], v_["ci"]
        return ((ri ^ ci) < 2 * b) & ((ri & b) != 0) & ((ci & b) == 0)

    def setup():
        ri = lax.broadcasted_iota(jnp.int32, (rows, CHUNK), 0) & (CHUNK - 1)
        ci = lax.broadcasted_iota(jnp.int32, (rows, CHUNK), 1)
        r1 = lax.broadcasted_iota(jnp.int32, (CHUNK, CHUNK), 0)
        c1 = lax.broadcasted_iota(jnp.int32, (CHUNK, CHUNK), 1)
        tril = jnp.where(r1 >= c1, 1.0, 0.0).astype(F32)
        v_.update(ri=ri, ci=ci, ts=[], ps=[], rhss=[])
        ba = ba_ref[...]
        sp = ba + hp_ref[1:2, :]
        softplus = jnp.maximum(sp, 0.0) + jnp.log(1.0 + jnp.exp(-jnp.abs(sp)))
        g_all = -jnp.exp(hp_ref[0:1, :]) * softplus
        gc_all = jnp.concatenate(
            [jnp.dot(tril, _rows(g_all, g), preferred_element_type=F32, precision=lax.Precision.HIGHEST)
             for g in range(groups)], axis=0)
        gc_last = jnp.concatenate(
            [jnp.broadcast_to(gc_all[(g + 1) * CHUNK - 1:(g + 1) * CHUNK, :], (CHUNK, LANES))
             for g in range(groups)], axis=0)
        v_.update(beta_all=_sigmoid(ba), g_all=g_all, gc_all=gc_all, e_gc=jnp.exp(gc_all),
                  e_rev=jnp.exp(gc_last - gc_all), e_last=jnp.exp(gc_last))

    def prep(h):
        ri, ci = v_["ri"], v_["ci"]
        q = x_ref[:, h * B_HEAD_DIM:(h + 1) * B_HEAD_DIM]
        k = x_ref[:, B_WIDTH + h * B_HEAD_DIM:B_WIDTH + (h + 1) * B_HEAD_DIM]
        v = x_ref[:, 2 * B_WIDTH + h * B_HEAD_DIM:2 * B_WIDTH + (h + 1) * B_HEAD_DIM]
        q = q * (lax.rsqrt(jnp.sum(q * q, axis=-1, keepdims=True) + EPS) * (B_HEAD_DIM ** -0.5))
        k = k * lax.rsqrt(jnp.sum(k * k, axis=-1, keepdims=True) + EPS)
        beta = _lane_bcast(v_["beta_all"], h, B_HEAD_DIM)
        egc = _lane_bcast(v_["e_gc"], B_HEADS + h, B_HEAD_DIM)
        gmat = _lane_bcast(v_["g_all"], B_HEADS + h, CHUNK)
        gc_col = _lane_bcast(v_["gc_all"], B_HEADS + h, CHUNK)
        gc_row = jnp.sum(jnp.where(ri <= ci, gmat, 0.0).reshape(groups, CHUNK, CHUNK), axis=1, keepdims=True)
        gc_row = jnp.broadcast_to(gc_row, (groups, CHUNK, CHUNK)).reshape(rows, CHUNK)
        decay = jnp.exp(jnp.where(ri >= ci, gc_col - gc_row, -jnp.inf))
        kb, qb = k.astype(BF16), q.astype(BF16)
        kk = jnp.concatenate([_mm_nt(_rows(kb, g), _rows(kb, g)) for g in range(groups)], axis=0)
        qk = jnp.concatenate([_mm_nt(_rows(qb, g), _rows(kb, g)) for g in range(groups)], axis=0)
        a = jnp.where(ri > ci, beta[:, :CHUNK] * kk * decay, 0.0)
        eye = jnp.where(ri == ci, 1.0, 0.0).astype(F32)
        v_["ts"].append(eye - jnp.where(pair_mask(1), a, 0.0))
        v_["ps"].append(a)
        v_["rhss"].append(jnp.concatenate([v * beta, k * (beta * egc)], axis=-1))
        k_dec = k * _lane_bcast(v_["e_rev"], B_HEADS + h, B_HEAD_DIM)
        qd_scr[h] = (q * egc).astype(BF16)
        qk_scr[h] = (qk * decay).astype(BF16)
        kdt_scr[h] = jnp.concatenate([_rows(k_dec, g).T for g in range(groups)], axis=0).astype(BF16)
        e_last = _lane_bcast(v_["e_last"], B_HEADS + h, B_HEAD_DIM)
        gl_scr[h] = jnp.concatenate([e_last[g * CHUNK:g * CHUNK + SUBLANES] for g in range(groups)], axis=0)

    def level(b):
        below = pair_mask(b)
        ns = [jnp.where(below, a, 0.0) for a in v_["ps"]]
        ws = [_bmm(n, t, groups) for n, t in zip(ns, v_["ts"])]
        v_["ts"] = [t - _bmm(t, w, groups) for t, w in zip(v_["ts"], ws)]

    def solve(h):
        sol = _bmm(v_["ts"][h], v_["rhss"][h], groups)
        u_scr[h] = sol[:, :B_HEAD_DIM]
        w_scr[h] = sol[:, B_HEAD_DIM:].astype(BF16)

    def step(c):
        blk = lambda ref, h, n=CHUNK: ref[h, c * n:(c + 1) * n, :]
        s_old = [state[h] for h in heads]
        r1s = [_mm(jnp.concatenate([blk(w_scr, h), blk(qd_scr, h)], axis=0), s) for h, s in zip(heads, s_old)]
        us = [blk(u_scr, h) - r[:CHUNK] for h, r in zip(heads, r1s)]
        r2s = [_mm(jnp.concatenate([blk(qk_scr, h), blk(kdt_scr, h, B_HEAD_DIM)], axis=0), u)
               for h, u in zip(heads, us)]
        for h, s, ra, rb in zip(heads, s_old, r1s, r2s):
            gl = blk(gl_scr, h, SUBLANES)
            state[h] = s * jnp.concatenate([gl] * (B_HEAD_DIM // SUBLANES), axis=0) + rb[CHUNK:]
            o_scr[c * CHUNK:(c + 1) * CHUNK, h * B_HEAD_DIM:(h + 1) * B_HEAD_DIM] = ra[CHUNK:] + rb[:CHUNK]

    def finish(h):
        cs = slice(h * B_HEAD_DIM, (h + 1) * B_HEAD_DIM)
        o = o_scr[:, cs]
        o = o * lax.rsqrt(jnp.mean(o * o, axis=-1, keepdims=True) + EPS) * ng_ref[...] * _silu(z_ref[:, cs])
        o_ref[:, cs] = o.astype(o_ref.dtype)

    levels = []
    b = 2
    while b < CHUNK:
        levels.append(functools.partial(level, b))
        b *= 2
    return ([setup] + [functools.partial(prep, h) for h in heads] + levels
            + [functools.partial(solve, h) for h in heads] + [functools.partial(step, c) for c in range(groups)]
            + [functools.partial(finish, h) for h in heads])


def _mixing_kernel(x_ref, mod_ref, g_ref, wm_ref, wt_ref, wc_ref, brow_ref, hp_ref, ng_ref,
                   ya_ref, yb_ref, og_ref,
                   xext, q_scr, k_scr, v_scr, bias_scr, wg_scr, qkv_scr, z_scr, ba_scr,
                   u_scr, w_scr, qd_scr, qk_scr, kdt_scr, gl_scr, o_scr, state,
                   *, tiles_per_batch):
    tm = x_ref.shape[0]
    a3, b3 = 3 * A_WIDTH, 3 * B_WIDTH
    first = pl.program_id(0) % tiles_per_batch == 0

    @pl.when(pl.program_id(0) == 0)
    def _():
        for hd in range(A_HEADS):
            tiled = jnp.broadcast_to(brow_ref[hd:hd + 1, :], (CHUNK, BIAS_ROW))
            rolled = pltpu.roll(tiled, BIAS_ROW - (CHUNK - 1), 1, stride=1, stride_axis=0)
            bias_scr[hd * CHUNK:(hd + 1) * CHUNK, :] = rolled[:, :A_BAND]
        k_scr[...] = jnp.zeros_like(k_scr)
        v_scr[...] = jnp.zeros_like(v_scr)
        wg_scr[...] = wt_ref[:, 2 * B_HEADS:2 * B_HEADS + wg_scr.shape[1]]

    @pl.when(first)
    def _():
        xext[0:SUBLANES, :] = jnp.zeros((SUBLANES, b3), F32)
        state[...] = jnp.zeros_like(state)

    k_scr[0:tm, :] = k_scr[tm:2 * tm, :]
    v_scr[0:tm, :] = v_scr[tm:2 * tm, :]

    h = _norm_mod(x_ref[...], g_ref[...], mod_ref[0:1, :], mod_ref[1:2, :]).astype(BF16)

    def proj(w_ref, c0, cw):
        return jnp.dot(h, w_ref[:, c0:c0 + cw], preferred_element_type=F32)

    scale = A_HEAD_DIM ** -0.5
    for c0 in range(0, A_WIDTH, COL_TILE):
        q_scr[:, c0:c0 + COL_TILE] = proj(wm_ref, c0, COL_TILE).astype(BF16) * scale
        k_scr[tm:2 * tm, c0:c0 + COL_TILE] = proj(wm_ref, A_WIDTH + c0, COL_TILE).astype(BF16)
        v_scr[tm:2 * tm, c0:c0 + COL_TILE] = proj(wm_ref, 2 * A_WIDTH + c0, COL_TILE).astype(BF16)

    def conv_tile(c0):
        full = xext[:, c0:c0 + LANES]
        acc = wc_ref[CONV_K - 1:CONV_K, c0:c0 + LANES] * full[SUBLANES:]
        for s in range(1, CONV_K):
            shifted = pltpu.roll(full, s, 0)[SUBLANES:]
            acc = acc + wc_ref[CONV_K - 1 - s:CONV_K - s, c0:c0 + LANES] * shifted
        qkv_scr[:, c0:c0 + LANES] = _silu(acc)

    def raw_job(r0):
        xext[SUBLANES:SUBLANES + tm, r0:r0 + COL_TILE] = proj(wm_ref, a3 + r0, COL_TILE)
        for c0 in range(r0, r0 + COL_TILE, LANES):
            conv_tile(c0)

    def plain_job(o_ref, w_ref, wc0, oc0):
        cw = min(COL_TILE, o_ref.shape[1])
        o_ref[:, oc0:oc0 + cw] = proj(w_ref, wc0, cw).astype(o_ref.dtype)

    attn = _attention_jobs(q_scr, k_scr, v_scr, bias_scr, ya_ref, first)
    gdn = _gdn_jobs(qkv_scr, z_scr, ba_scr, hp_ref, ng_ref, yb_ref,
                    u_scr, w_scr, qd_scr, qk_scr, kdt_scr, gl_scr, o_scr, state)
    feed = [functools.partial(raw_job, r0) for r0 in range(0, b3, COL_TILE)]
    feed += [functools.partial(plain_job, z_scr, wm_ref, a3 + b3 + c0, c0) for c0 in range(0, B_WIDTH, COL_TILE)]
    feed += [functools.partial(plain_job, ba_scr, wt_ref, 0, 0)]
    gates = [functools.partial(plain_job, og_ref, wg_scr, c0, c0) for c0 in range(0, og_ref.shape[1], COL_TILE)]
    n_early = len(attn) * len(feed) // (len(feed) + len(gates))
    for job in _interleave(feed, attn[:n_early]) + _interleave(gates, gdn, attn[n_early:]):
        job()
    xext[0:SUBLANES, :] = xext[tm:tm + SUBLANES, :]


def _token_mixing(x2, mod_l, gain, w_main, w_tail, w_conv, bias_rows, head_params, norm_g, layer, rows_per_batch):
    m, d = x2.shape
    tm = ROW_TILE
    tpb = rows_per_batch // tm
    groups = tm // CHUNK
    widths = (A_WIDTH, B_WIDTH, 2 * d)
    dtypes = (BF16, BF16, F32)
    row_spec = lambda w: pl.BlockSpec((tm, w), lambda i: (i, 0))
    head_scr = lambda r, w, t: pltpu.VMEM((B_HEADS, r, w), t)
    return pl.pallas_call(
        functools.partial(_mixing_kernel, tiles_per_batch=tpb),
        grid=(m // tm,),
        in_specs=[
            row_spec(d),
            pl.BlockSpec((None, N_MOD, d), lambda i: (i // tpb, 0, 0)),
            _resident((None, 1, d), lambda i: (layer, 0, 0)),
            _resident((None, d, w_main.shape[-1]), lambda i: (layer, 0, 0)),
            _resident((None, d, w_tail.shape[-1]), lambda i: (layer, 0, 0)),
            _resident((None, CONV_K, 3 * B_WIDTH), lambda i: (layer, 0, 0)),
            _resident((None, A_HEADS, BIAS_ROW), lambda i: (layer, 0, 0)),
            _resident((None, 2, LANES), lambda i: (layer, 0, 0)),
            _resident((None, 1, B_HEAD_DIM), lambda i: (layer, 0, 0)),
        ],
        out_specs=[row_spec(w) for w in widths],
        out_shape=[jax.ShapeDtypeStruct((m, w), t) for w, t in zip(widths, dtypes)],
        scratch_shapes=[pltpu.VMEM((tm + SUBLANES, 3 * B_WIDTH), F32),
                        pltpu.VMEM((tm, A_WIDTH), BF16),
                        pltpu.VMEM((2 * tm, A_WIDTH), BF16), pltpu.VMEM((2 * tm, A_WIDTH), BF16),
                        pltpu.VMEM((A_HEADS * CHUNK, A_BAND), F32),
                        pltpu.VMEM((d, 2 * d), BF16),
                        pltpu.VMEM((tm, 3 * B_WIDTH), F32), pltpu.VMEM((tm, B_WIDTH), F32),
                        pltpu.VMEM((tm, LANES), F32),
                        head_scr(tm, B_HEAD_DIM, F32), head_scr(tm, B_HEAD_DIM, BF16),
                        head_scr(tm, B_HEAD_DIM, BF16), head_scr(tm, CHUNK, BF16),
                        head_scr(groups * B_HEAD_DIM, CHUNK, BF16), head_scr(groups * SUBLANES, B_HEAD_DIM, F32),
                        pltpu.VMEM((tm, B_WIDTH), F32),
                        pltpu.VMEM((B_HEADS, B_HEAD_DIM, B_HEAD_DIM), F32)],
        compiler_params=_params("arbitrary"),
        name="token_mixing",
    )(x2, mod_l, gain, w_main, w_tail, w_conv, bias_rows, head_params, norm_g)


def _mix_ffn_kernel(x_ref, ya_ref, yb_ref, gt_ref, mod_ref, g_ref, fg_ref, wa_ref, wb_ref, wm_ref, wi_ref, wo_ref,
                    o_ref, x1_ref, acc_ref, *, final):
    d = x_ref.shape[-1]
    merged = (_sigmoid(gt_ref[:, 0:d]) * jnp.dot(ya_ref[...], wa_ref[...], preferred_element_type=F32)
              + _sigmoid(gt_ref[:, d:2 * d]) * jnp.dot(yb_ref[...], wb_ref[...], preferred_element_type=F32))
    x1_ref[...] = x_ref[...] + mod_ref[2:3, :] * _mm(merged, wm_ref[...])
    h = _norm_mod(x1_ref[...], g_ref[...], mod_ref[3:4, :], mod_ref[4:5, :]).astype(BF16)
    hidden = wo_ref.shape[0]
    for j in range(hidden // FFN_CHUNK):
        js = slice(j * FFN_CHUNK, (j + 1) * FFN_CHUNK)
        gate = jnp.dot(h, wi_ref[:, js], preferred_element_type=F32)
        up = jnp.dot(h, wi_ref[:, hidden + j * FFN_CHUNK:hidden + (j + 1) * FFN_CHUNK],
                     preferred_element_type=F32)
        act = (_silu(gate) * up).astype(BF16)
        part = jnp.dot(act, wo_ref[js, :], preferred_element_type=F32)
        if j == 0:
            acc_ref[...] = part
        else:
            acc_ref[...] += part
    out = x1_ref[...] + mod_ref[5:6, :] * acc_ref[...]
    if final:
        out = out * lax.rsqrt(jnp.mean(out * out, axis=-1, keepdims=True) + EPS) * fg_ref[...]
    o_ref[...] = out


def _mix_ffn(x2, ya, yb, gates, mod_l, gain, final_g, wa, wb, wm, wi, wo, layer, rows_per_batch, final):
    m, d = x2.shape
    tm = ROW_TILE
    tpb = rows_per_batch // tm
    hidden = wo.shape[1]
    row_spec = lambda w: pl.BlockSpec((tm, w), lambda i: (i, 0))
    return pl.pallas_call(
        functools.partial(_mix_ffn_kernel, final=final),
        grid=(m // tm,),
        in_specs=[row_spec(d), row_spec(A_WIDTH), row_spec(B_WIDTH), row_spec(2 * d),
                  pl.BlockSpec((None, N_MOD, d), lambda i: (i // tpb, 0, 0)),
                  _resident((None, 1, d), lambda i: (layer, 0, 0)),
                  _resident((1, d), lambda i: (0, 0)),
                  _resident((None, A_WIDTH, d), lambda i: (layer, 0, 0)),
                  _resident((None, B_WIDTH, d), lambda i: (layer, 0, 0)),
                  _resident((None, d, d), lambda i: (layer, 0, 0)),
                  _resident((None, d, 2 * hidden), lambda i: (layer, 0, 0)),
                  _resident((None, hidden, d), lambda i: (layer, 0, 0))],
        out_specs=row_spec(d),
        out_shape=jax.ShapeDtypeStruct((m, d), F32),
        scratch_shapes=[pltpu.VMEM((tm, d), F32), pltpu.VMEM((tm, d), F32)],
        compiler_params=_params("arbitrary"),
        name="merge_swiglu",
    )(x2, ya, yb, gates, mod_l, gain, final_g, wa, wb, wm, wi, wo)


def _bias_rows(rel_table):
    t = rel_table.astype(F32)
    lo = A_MAX_REL - (CHUNK - 1)
    ramp = t[..., lo:][..., ::-1]
    n_far = CHUNK + A_BAND - 1 - ramp.shape[-1]
    far = jnp.broadcast_to(t[..., -1:], t.shape[:-1] + (n_far,))
    pad = jnp.zeros(t.shape[:-1] + (BIAS_ROW - (CHUNK + A_BAND - 1),), F32)
    return jnp.concatenate([far, ramp, pad], axis=-1)


def kernel(x, c, w_ada, b_ada, norm1_g, norm2_g, w_in, rel_table, w_conv, a_log, dt_bias, gdn_norm_g,
           w_branch_a, w_branch_b, w_out, w_ffn_in, w_ffn_out, final_g):
    bsz, seq, d = x.shape
    depth = w_ada.shape[0]
    assert seq % ROW_TILE == 0 and d % LANES == 0

    mod = _modulation(c, w_ada, b_ada)
    n_main = 3 * A_WIDTH + 4 * B_WIDTH
    n_tail = w_in.shape[-1] - n_main
    w_main = w_in[..., :n_main].astype(BF16)
    w_tail = jnp.pad(w_in[..., n_main:].astype(BF16), ((0, 0), (0, 0), (0, -n_tail % LANES)))
    wi, wo_ffn = w_ffn_in.astype(BF16), w_ffn_out.astype(BF16)
    wa, wb, wo = w_branch_a.astype(BF16), w_branch_b.astype(BF16), w_out.astype(BF16)
    bias_rows = _bias_rows(rel_table)
    lane_pad = jnp.zeros((depth, LANES - 2 * B_HEADS), F32)
    zeros_h = jnp.zeros((depth, B_HEADS), F32)
    head_params = jnp.stack([jnp.concatenate([zeros_h, a_log, lane_pad], axis=-1),
                             jnp.concatenate([zeros_h, dt_bias, lane_pad], axis=-1)], axis=1)
    g1 = norm1_g.reshape(depth, 1, d)
    g2 = norm2_g.reshape(depth, 1, d)
    ng = gdn_norm_g.reshape(depth, 1, B_HEAD_DIM)

    x2 = x.reshape(bsz * seq, d)
    for l in range(depth):
        ya, yb, gates = _token_mixing(x2, mod[l], g1, w_main, w_tail, w_conv, bias_rows, head_params, ng, l, seq)
        x2 = _mix_ffn(x2, ya, yb, gates, mod[l], g2, final_g.reshape(1, d),
                      wa, wb, wo, wi, wo_ffn, l, seq, final=(l == depth - 1))
    return x2.reshape(bsz, seq, d)
```

```python
import functools

import jax
import jax.numpy as jnp
from jax import lax
from jax.experimental import pallas as pl
from jax.experimental.pallas import tpu as pltpu

F32 = jnp.float32
BF16 = jnp.bfloat16

EPS = 1e-6
CHUNK = 64
A_HEADS = 8
A_HEAD_DIM = 64
A_WIDTH = A_HEADS * A_HEAD_DIM
A_PAST_CHUNKS = 8
A_BAND = (A_PAST_CHUNKS + 1) * CHUNK
A_MAX_REL = 128
B_HEADS = 4
B_HEAD_DIM = 128
B_WIDTH = B_HEADS * B_HEAD_DIM
CONV_K = 4
N_MOD = 6
LANES = 128
SUBLANES = 8
VMEM_LIMIT = 60 * 1024 * 1024

ROW_TILE = A_PAST_CHUNKS * CHUNK
COL_TILE = 256
FFN_CHUNK = 256
ATTN_PAIRS_PER_PHASE = 4
BIAS_ROW = 5 * LANES


def _params(*sem):
    return pltpu.CompilerParams(dimension_semantics=sem, vmem_limit_bytes=VMEM_LIMIT)


def _resident(block_shape, index_map):
    return pl.BlockSpec(block_shape, index_map, pipeline_mode=pl.Buffered(1))


def _mm(a, b):
    return jnp.dot(a.astype(BF16), b.astype(BF16), preferred_element_type=F32)


def _mm_nt(a, b):
    return lax.dot_general(a.astype(BF16), b.astype(BF16), (((1,), (1,)), ((), ())),
                           preferred_element_type=F32)


def _rows(x, g, n=CHUNK):
    return x[g * n:(g + 1) * n]


def _bmm(a, b, groups, rows_a=CHUNK, rows_b=CHUNK):
    return jnp.concatenate([_mm(_rows(a, g, rows_a), _rows(b, g, rows_b)) for g in range(groups)], axis=0)


def _sigmoid(x):
    return 1.0 / (1.0 + jnp.exp(-x))


def _silu(x):
    return x * _sigmoid(x)


def _norm_mod(x, gain, shift, scale):
    ms = jnp.mean(x * x, axis=-1, keepdims=True)
    return x * lax.rsqrt(ms + EPS) * gain * (1.0 + scale) + shift


def _lane_bcast(x, lane, width):
    return jnp.broadcast_to(x[:, lane:lane + 1], (x.shape[0], width))


def _interleave(*lists):
    tagged = sorted(((i + 0.5) / len(l), n, i) for n, l in enumerate(lists) for i in range(len(l)))
    return [lists[n][i] for _, n, i in tagged]


def _mod_kernel(c_ref, w_ref, b_ref, o_ref):
    cond = _silu(c_ref[...])
    o_ref[...] = _mm(cond, w_ref[...]) + b_ref[...]


def _modulation(c, w_ada, b_ada):
    depth, d, n = w_ada.shape
    bsz = c.shape[0]
    rows = -(-bsz // SUBLANES) * SUBLANES
    c_pad = jnp.zeros((rows, d), F32).at[:bsz].set(c)
    tn = n // 4
    out = pl.pallas_call(
        _mod_kernel,
        grid=(depth, n // tn),
        in_specs=[
            pl.BlockSpec((rows, d), lambda l, j: (0, 0)),
            pl.BlockSpec((None, d, tn), lambda l, j: (l, 0, j)),
            pl.BlockSpec((None, 1, tn), lambda l, j: (l, 0, j)),
        ],
        out_specs=pl.BlockSpec((None, rows, tn), lambda l, j: (l, 0, j)),
        out_shape=jax.ShapeDtypeStruct((depth, rows, n), F32),
        compiler_params=_params("arbitrary", "arbitrary"),
        name="adaln_modulation",
    )(c_pad, w_ada, b_ada.reshape(depth, 1, n))
    return out[:, :bsz].reshape(depth, bsz, N_MOD, d)


def _attention_jobs(q_scr, k_scr, v_scr, bias_scr, ya_ref, first):
    tm = q_scr.shape[0]
    pairs = A_WIDTH // LANES
    heads_per_pair = LANES // A_HEAD_DIM
    past = A_PAST_CHUNKS * CHUNK
    lane = lax.broadcasted_iota(jnp.int32, (CHUNK, LANES), 1)
    col = lax.broadcasted_iota(jnp.int32, (1, A_BAND), 1)
    live = {}

    def scores(r0, p0):
        base = r0 + (tm - past)
        parts = []
        for p in range(p0, p0 + ATTN_PAIRS_PER_PHASE):
            cs = slice(p * LANES, (p + 1) * LANES)
            q2 = q_scr[r0:r0 + CHUNK, cs]
            zero = jnp.zeros_like(q2)
            qs = jnp.concatenate(
                [jnp.where((lane >= hh * A_HEAD_DIM) & (lane < (hh + 1) * A_HEAD_DIM), q2, zero)
                 for hh in range(heads_per_pair)], axis=0)
            parts.append(_mm_nt(qs, k_scr[base:base + A_BAND, cs]))
        valid = jnp.logical_or(col + base >= tm, jnp.logical_not(first))
        bias = bias_scr[p0 * LANES:(p0 + ATTN_PAIRS_PER_PHASE) * LANES, :]
        live["s"] = jnp.concatenate(parts, axis=0) + bias + jnp.where(valid, 0.0, -1e30)

    def softmax(r0, p0):
        s = live.pop("s")
        e = jnp.exp(s - jnp.max(s, axis=-1, keepdims=True))
        live["p"] = (e.astype(BF16), 1.0 / jnp.sum(e, axis=-1, keepdims=True))

    def output(r0, p0):
        pb, rinv = live.pop("p")
        base = r0 + (tm - past)
        for n, p in enumerate(range(p0, p0 + ATTN_PAIRS_PER_PHASE)):
            cs = slice(p * LANES, (p + 1) * LANES)
            rs = slice(n * LANES, (n + 1) * LANES)
            o = _mm(pb[rs], v_scr[base:base + A_BAND, cs]) * rinv[rs]
            o2 = jnp.where(lane < A_HEAD_DIM, o[:CHUNK], o[CHUNK:])
            ya_ref[r0:r0 + CHUNK, cs] = o2.astype(ya_ref.dtype)

    return [functools.partial(f, r0, p0) for r0 in range(0, tm, CHUNK)
            for p0 in range(0, pairs, ATTN_PAIRS_PER_PHASE) for f in (scores, softmax, output)]


def _gdn_jobs(x_ref, z_ref, ba_ref, hp_ref, ng_ref, o_ref,
              u_scr, w_scr, qd_scr, qk_scr, kdt_scr, gl_scr, o_scr, state):
    rows = x_ref.shape[0]
    groups = rows // CHUNK
    heads = range(B_HEADS)
    v_ = {}

    def pair_mask(b):
        ri, ci = v_["ri"], v_["ci"]
        return ((ri ^ ci) < 2 * b) & ((ri & b) != 0) & ((ci & b) == 0)

    def setup():
        ri = lax.broadcasted_iota(jnp.int32, (rows, CHUNK), 0) & (CHUNK - 1)
        ci = lax.broadcasted_iota(jnp.int32, (rows, CHUNK), 1)
        r1 = lax.broadcasted_iota(jnp.int32, (CHUNK, CHUNK), 0)
        c1 = lax.broadcasted_iota(jnp.int32, (CHUNK, CHUNK), 1)
        tril = jnp.where(r1 >= c1, 1.0, 0.0).astype(F32)
        v_.update(ri=ri, ci=ci, ts=[], ps=[], rhss=[])
        ba = ba_ref[...]
        sp = ba + hp_ref[1:2, :]
        softplus = jnp.maximum(sp, 0.0) + jnp.log(1.0 + jnp.exp(-jnp.abs(sp)))
        g_all = -jnp.exp(hp_ref[0:1, :]) * softplus
        gc_all = jnp.concatenate(
            [jnp.dot(tril, _rows(g_all, g), preferred_element_type=F32, precision=lax.Precision.HIGHEST)
             for g in range(groups)], axis=0)
        gc_last = jnp.concatenate(
            [jnp.broadcast_to(gc_all[(g + 1) * CHUNK - 1:(g + 1) * CHUNK, :], (CHUNK, LANES))
             for g in range(groups)], axis=0)
        v_.update(beta_all=_sigmoid(ba), g_all=g_all, gc_all=gc_all, e_gc=jnp.exp(gc_all),
                  e_rev=jnp.exp(gc_last - gc_all), e_last=jnp.exp(gc_last))

    def prep(h):
        ri, ci = v_["ri"], v_["ci"]
        q = x_ref[:, h * B_HEAD_DIM:(h + 1) * B_HEAD_DIM]
        k = x_ref[:, B_WIDTH + h * B_HEAD_DIM:B_WIDTH + (h + 1) * B_HEAD_DIM]
        v = x_ref[:, 2 * B_WIDTH + h * B_HEAD_DIM:2 * B_WIDTH + (h + 1) * B_HEAD_DIM]
        q = q * (lax.rsqrt(jnp.sum(q * q, axis=-1, keepdims=True) + EPS) * (B_HEAD_DIM ** -0.5))
        k = k * lax.rsqrt(jnp.sum(k * k, axis=-1, keepdims=True) + EPS)
        beta = _lane_bcast(v_["beta_all"], h, B_HEAD_DIM)
        egc = _lane_bcast(v_["e_gc"], B_HEADS + h, B_HEAD_DIM)
        gmat = _lane_bcast(v_["g_all"], B_HEADS + h, CHUNK)
        gc_col = _lane_bcast(v_["gc_all"], B_HEADS + h, CHUNK)
        gc_row = jnp.sum(jnp.where(ri <= ci, gmat, 0.0).reshape(groups, CHUNK, CHUNK), axis=1, keepdims=True)
        gc_row = jnp.broadcast_to(gc_row, (groups, CHUNK, CHUNK)).reshape(rows, CHUNK)
        decay = jnp.exp(jnp.where(ri >= ci, gc_col - gc_row, -jnp.inf))
        kb, qb = k.astype(BF16), q.astype(BF16)
        kk = jnp.concatenate([_mm_nt(_rows(kb, g), _rows(kb, g)) for g in range(groups)], axis=0)
        qk = jnp.concatenate([_mm_nt(_rows(qb, g), _rows(kb, g)) for g in range(groups)], axis=0)
        a = jnp.where(ri > ci, beta[:, :CHUNK] * kk * decay, 0.0)
        eye = jnp.where(ri == ci, 1.0, 0.0).astype(F32)
        v_["ts"].append(eye - jnp.where(pair_mask(1), a, 0.0))
        v_["ps"].append(a)
        v_["rhss"].append(jnp.concatenate([v * beta, k * (beta * egc)], axis=-1))
        k_dec = k * _lane_bcast(v_["e_rev"], B_HEADS + h, B_HEAD_DIM)
        qd_scr[h] = (q * egc).astype(BF16)
        qk_scr[h] = (qk * decay).astype(BF16)
        kdt_scr[h] = jnp.concatenate([_rows(k_dec, g).T for g in range(groups)], axis=0).astype(BF16)
        e_last = _lane_bcast(v_["e_last"], B_HEADS + h, B_HEAD_DIM)
        gl_scr[h] = jnp.concatenate([e_last[g * CHUNK:g * CHUNK + SUBLANES] for g in range(groups)], axis=0)

    def level(b):
        below = pair_mask(b)
        ns = [jnp.where(below, a, 0.0) for a in v_["ps"]]
        ws = [_bmm(n, t, groups) for n, t in zip(ns, v_["ts"])]
        v_["ts"] = [t - _bmm(t, w, groups) for t, w in zip(v_["ts"], ws)]

    def solve(h):
        sol = _bmm(v_["ts"][h], v_["rhss"][h], groups)
        u_scr[h] = sol[:, :B_HEAD_DIM]
        w_scr[h] = sol[:, B_HEAD_DIM:].astype(BF16)

    def step(c):
        blk = lambda ref, h, n=CHUNK: ref[h, c * n:(c + 1) * n, :]
        s_old = [state[h] for h in heads]
        r1s = [_mm(jnp.concatenate([blk(w_scr, h), blk(qd_scr, h)], axis=0), s) for h, s in zip(heads, s_old)]
        us = [blk(u_scr, h) - r[:CHUNK] for h, r in zip(heads, r1s)]
        r2s = [_mm(jnp.concatenate([blk(qk_scr, h), blk(kdt_scr, h, B_HEAD_DIM)], axis=0), u)
               for h, u in zip(heads, us)]
        for h, s, ra, rb in zip(heads, s_old, r1s, r2s):
            gl = blk(gl_scr, h, SUBLANES)
            state[h] = s * jnp.concatenate([gl] * (B_HEAD_DIM // SUBLANES), axis=0) + rb[CHUNK:]
            o_scr[c * CHUNK:(c + 1) * CHUNK, h * B_HEAD_DIM:(h + 1) * B_HEAD_DIM] = ra[CHUNK:] + rb[:CHUNK]

    def finish(h):
        cs = slice(h * B_HEAD_DIM, (h + 1) * B_HEAD_DIM)
        o = o_scr[:, cs]
        o = o * lax.rsqrt(jnp.mean(o * o, axis=-1, keepdims=True) + EPS) * ng_ref[...] * _silu(z_ref[:, cs])
        o_ref[:, cs] = o.astype(o_ref.dtype)

    levels = []
    b = 2
    while b < CHUNK:
        levels.append(functools.partial(level, b))
        b *= 2
    return ([setup] + [functools.partial(prep, h) for h in heads] + levels
            + [functools.partial(solve, h) for h in heads] + [functools.partial(step, c) for c in range(groups)]
            + [functools.partial(finish, h) for h in heads])


def _mixing_kernel(x0_ref, mod0_ref, xn_ref, modn_ref, g_ref, wm_ref, wt_ref, wc_ref, brow_ref, hp_ref, ng_ref,
                   ya_ref, yb_ref, og_ref,
                   h_scr, hn_scr, xext, q_scr, k_scr, v_scr, bias_scr, wg_scr, qkv_scr, z_scr, ba_scr,
                   u_scr, w_scr, qd_scr, qk_scr, kdt_scr, gl_scr, o_scr, state,
                   *, tiles_per_batch):
    tm = xn_ref.shape[0]
    a3, b3 = 3 * A_WIDTH, 3 * B_WIDTH
    first = pl.program_id(0) % tiles_per_batch == 0

    @pl.when(pl.program_id(0) == 0)
    def _():
        for hd in range(A_HEADS):
            tiled = jnp.broadcast_to(brow_ref[hd:hd + 1, :], (CHUNK, BIAS_ROW))
            rolled = pltpu.roll(tiled, BIAS_ROW - (CHUNK - 1), 1, stride=1, stride_axis=0)
            bias_scr[hd * CHUNK:(hd + 1) * CHUNK, :] = rolled[:, :A_BAND]
        k_scr[...] = jnp.zeros_like(k_scr)
        v_scr[...] = jnp.zeros_like(v_scr)
        wg_scr[...] = wt_ref[:, 2 * B_HEADS:2 * B_HEADS + wg_scr.shape[1]]

    @pl.when(first)
    def _():
        xext[0:SUBLANES, :] = jnp.zeros((SUBLANES, b3), F32)
        state[...] = jnp.zeros_like(state)

    k_scr[0:tm, :] = k_scr[tm:2 * tm, :]
    v_scr[0:tm, :] = v_scr[tm:2 * tm, :]

    def norm_next(x_ref, mod_ref):
        hn_scr[...] = _norm_mod(x_ref[...], g_ref[...], mod_ref[0:1, :], mod_ref[1:2, :]).astype(BF16)

    @pl.when(pl.program_id(0) == 0)
    def _():
        norm_next(x0_ref, mod0_ref)

    h_scr[...] = hn_scr[...]
    norm_next(xn_ref, modn_ref)

    def proj(w_ref, c0, cw):
        return jnp.dot(h_scr[...], w_ref[:, c0:c0 + cw], preferred_element_type=F32)

    scale = A_HEAD_DIM ** -0.5
    for c0 in range(0, A_WIDTH, COL_TILE):
        q_scr[:, c0:c0 + COL_TILE] = proj(wm_ref, c0, COL_TILE).astype(BF16) * scale
        k_scr[tm:2 * tm, c0:c0 + COL_TILE] = proj(wm_ref, A_WIDTH + c0, COL_TILE).astype(BF16)
        v_scr[tm:2 * tm, c0:c0 + COL_TILE] = proj(wm_ref, 2 * A_WIDTH + c0, COL_TILE).astype(BF16)

    def conv_tile(c0):
        full = xext[:, c0:c0 + LANES]
        acc = wc_ref[CONV_K - 1:CONV_K, c0:c0 + LANES] * full[SUBLANES:]
        for s in range(1, CONV_K):
            shifted = pltpu.roll(full, s, 0)[SUBLANES:]
            acc = acc + wc_ref[CONV_K - 1 - s:CONV_K - s, c0:c0 + LANES] * shifted
        qkv_scr[:, c0:c0 + LANES] = _silu(acc)

    def raw_job(r0):
        xext[SUBLANES:SUBLANES + tm, r0:r0 + COL_TILE] = proj(wm_ref, a3 + r0, COL_TILE)
        for c0 in range(r0, r0 + COL_TILE, LANES):
            conv_tile(c0)

    def plain_job(o_ref, w_ref, wc0, oc0):
        cw = min(COL_TILE, o_ref.shape[1])
        o_ref[:, oc0:oc0 + cw] = proj(w_ref, wc0, cw).astype(o_ref.dtype)

    attn = _attention_jobs(q_scr, k_scr, v_scr, bias_scr, ya_ref, first)
    gdn = _gdn_jobs(qkv_scr, z_scr, ba_scr, hp_ref, ng_ref, yb_ref,
                    u_scr, w_scr, qd_scr, qk_scr, kdt_scr, gl_scr, o_scr, state)
    feed = [functools.partial(raw_job, r0) for r0 in range(0, b3, COL_TILE)]
    feed += [functools.partial(plain_job, z_scr, wm_ref, a3 + b3 + c0, c0) for c0 in range(0, B_WIDTH, COL_TILE)]
    feed += [functools.partial(plain_job, ba_scr, wt_ref, 0, 0)]
    gates = [functools.partial(plain_job, og_ref, wg_scr, c0, c0) for c0 in range(0, og_ref.shape[1], COL_TILE)]
    n_early = len(attn) * len(feed) // (len(feed) + len(gates))
    for job in _interleave(feed, attn[:n_early]) + _interleave(gates, gdn, attn[n_early:]):
        job()
    xext[0:SUBLANES, :] = xext[tm:tm + SUBLANES, :]


def _token_mixing(x2, mod_l, gain, w_all, w_tail, w_conv, bias_rows, head_params, norm_g, layer, rows_per_batch):
    m, d = x2.shape
    tm = ROW_TILE
    tpb = rows_per_batch // tm
    groups = tm // CHUNK
    widths = (A_WIDTH, B_WIDTH, 2 * d)
    dtypes = (BF16, BF16, BF16)
    row_spec = lambda w: pl.BlockSpec((tm, w), lambda i: (i, 0))
    head_scr = lambda r, w, t: pltpu.VMEM((B_HEADS, r, w), t)
    nxt = lambda i: jnp.minimum(i + 1, m // tm - 1)
    return pl.pallas_call(
        functools.partial(_mixing_kernel, tiles_per_batch=tpb),
        grid=(m // tm,),
        in_specs=[
            _resident((tm, d), lambda i: (0, 0)),
            _resident((None, N_MOD, d), lambda i: (0, 0, 0)),
            pl.BlockSpec((tm, d), lambda i: (nxt(i), 0)),
            pl.BlockSpec((None, N_MOD, d), lambda i: (nxt(i) // tpb, 0, 0)),
            _resident((None, 1, d), lambda i: (layer, 0, 0)),
            _resident((None, d, 3 * A_WIDTH + 4 * B_WIDTH), lambda i: (layer, 0, 0)),
            _resident((None, d, w_tail.shape[-1]), lambda i: (layer, 0, 0)),
            _resident((None, CONV_K, 3 * B_WIDTH), lambda i: (layer, 0, 0)),
            _resident((None, A_HEADS, BIAS_ROW), lambda i: (layer, 0, 0)),
            _resident((None, 2, LANES), lambda i: (layer, 0, 0)),
            _resident((None, 1, B_HEAD_DIM), lambda i: (layer, 0, 0)),
        ],
        out_specs=[row_spec(w) for w in widths],
        out_shape=[jax.ShapeDtypeStruct((m, w), t) for w, t in zip(widths, dtypes)],
        scratch_shapes=[pltpu.VMEM((tm, d), BF16), pltpu.VMEM((tm, d), BF16),
                        pltpu.VMEM((tm + SUBLANES, 3 * B_WIDTH), F32),
                        pltpu.VMEM((tm, A_WIDTH), BF16),
                        pltpu.VMEM((2 * tm, A_WIDTH), BF16), pltpu.VMEM((2 * tm, A_WIDTH), BF16),
                        pltpu.VMEM((A_HEADS * CHUNK, A_BAND), F32),
                        pltpu.VMEM((d, 2 * d), BF16),
                        pltpu.VMEM((tm, 3 * B_WIDTH), F32), pltpu.VMEM((tm, B_WIDTH), F32),
                        pltpu.VMEM((tm, LANES), F32),
                        head_scr(tm, B_HEAD_DIM, F32), head_scr(tm, B_HEAD_DIM, BF16),
                        head_scr(tm, B_HEAD_DIM, BF16), head_scr(tm, CHUNK, BF16),
                        head_scr(groups * B_HEAD_DIM, CHUNK, BF16), head_scr(groups * SUBLANES, B_HEAD_DIM, F32),
                        pltpu.VMEM((tm, B_WIDTH), F32),
                        pltpu.VMEM((B_HEADS, B_HEAD_DIM, B_HEAD_DIM), F32)],
        compiler_params=_params("arbitrary"),
        name="token_mixing",
    )(x2, mod_l, x2, mod_l, gain, w_all, w_tail, w_conv, bias_rows, head_params, norm_g)


def _mix_ffn_kernel(x_ref, ya_ref, yb_ref, gt_ref, mod_ref, g_ref, fg_ref, wa_ref, wb_ref, wm_ref, wi_ref, wo_ref,
                    o_ref, x1_ref, acc_ref, *, final):
    d = x_ref.shape[-1]
    gate_a = _sigmoid(gt_ref[:, 0:d].astype(F32))
    gate_b = _sigmoid(gt_ref[:, d:2 * d].astype(F32))
    merged = (gate_a * jnp.dot(ya_ref[...], wa_ref[...], preferred_element_type=F32)
              + gate_b * jnp.dot(yb_ref[...], wb_ref[...], preferred_element_type=F32))
    x1_ref[...] = x_ref[...] + mod_ref[2:3, :] * _mm(merged, wm_ref[...])
    h = _norm_mod(x1_ref[...], g_ref[...], mod_ref[3:4, :], mod_ref[4:5, :]).astype(BF16)
    hidden = wo_ref.shape[0]
    for j in range(hidden // FFN_CHUNK):
        js = slice(j * FFN_CHUNK, (j + 1) * FFN_CHUNK)
        gate = jnp.dot(h, wi_ref[:, js], preferred_element_type=F32)
        up = jnp.dot(h, wi_ref[:, hidden + j * FFN_CHUNK:hidden + (j + 1) * FFN_CHUNK],
                     preferred_element_type=F32)
        act = (_silu(gate) * up).astype(BF16)
        part = jnp.dot(act, wo_ref[js, :], preferred_element_type=F32)
        if j == 0:
            acc_ref[...] = part
        else:
            acc_ref[...] += part
    out = x1_ref[...] + mod_ref[5:6, :] * acc_ref[...]
    if final:
        out = out * lax.rsqrt(jnp.mean(out * out, axis=-1, keepdims=True) + EPS) * fg_ref[...]
    o_ref[...] = out


def _mix_ffn(x2, ya, yb, gates, mod_l, gain, final_g, wa, wb, wm, wi, wo, layer, rows_per_batch, final):
    m, d = x2.shape
    tm = ROW_TILE
    tpb = rows_per_batch // tm
    hidden = wo.shape[1]
    row_spec = lambda w: pl.BlockSpec((tm, w), lambda i: (i, 0))
    return pl.pallas_call(
        functools.partial(_mix_ffn_kernel, final=final),
        grid=(m // tm,),
        in_specs=[row_spec(d), row_spec(A_WIDTH), row_spec(B_WIDTH), row_spec(2 * d),
                  pl.BlockSpec((None, N_MOD, d), lambda i: (i // tpb, 0, 0)),
                  _resident((None, 1, d), lambda i: (layer, 0, 0)),
                  _resident((1, d), lambda i: (0, 0)),
                  _resident((None, A_WIDTH, d), lambda i: (layer, 0, 0)),
                  _resident((None, B_WIDTH, d), lambda i: (layer, 0, 0)),
                  _resident((None, d, d), lambda i: (layer, 0, 0)),
                  _resident((None, d, 2 * hidden), lambda i: (layer, 0, 0)),
                  _resident((None, hidden, d), lambda i: (layer, 0, 0))],
        out_specs=row_spec(d),
        out_shape=jax.ShapeDtypeStruct((m, d), F32),
        scratch_shapes=[pltpu.VMEM((tm, d), F32), pltpu.VMEM((tm, d), F32)],
        compiler_params=_params("arbitrary"),
        name="merge_swiglu",
    )(x2, ya, yb, gates, mod_l, gain, final_g, wa, wb, wm, wi, wo)


def _bias_rows(rel_table):
    t = rel_table.astype(F32)
    lo = A_MAX_REL - (CHUNK - 1)
    ramp = t[..., lo:][..., ::-1]
    n_far = CHUNK + A_BAND - 1 - ramp.shape[-1]
    far = jnp.broadcast_to(t[..., -1:], t.shape[:-1] + (n_far,))
    pad = jnp.zeros(t.shape[:-1] + (BIAS_ROW - (CHUNK + A_BAND - 1),), F32)
    return jnp.concatenate([far, ramp, pad], axis=-1)


def kernel(x, c, w_ada, b_ada, norm1_g, norm2_g, w_in, rel_table, w_conv, a_log, dt_bias, gdn_norm_g,
           w_branch_a, w_branch_b, w_out, w_ffn_in, w_ffn_out, final_g):
    bsz, seq, d = x.shape
    depth = w_ada.shape[0]
    assert seq % ROW_TILE == 0 and d % LANES == 0

    mod = _modulation(c, w_ada, b_ada)
    n_main = 3 * A_WIDTH + 4 * B_WIDTH
    n_tail = w_in.shape[-1] - n_main
    w_all = w_in.astype(BF16)
    w_tail = jnp.pad(w_all[..., n_main:], ((0, 0), (0, 0), (0, -n_tail % LANES)))
    wi, wo_ffn = w_ffn_in.astype(BF16), w_ffn_out.astype(BF16)
    wa, wb, wo = w_branch_a.astype(BF16), w_branch_b.astype(BF16), w_out.astype(BF16)
    bias_rows = _bias_rows(rel_table)
    lane_pad = jnp.zeros((depth, LANES - 2 * B_HEADS), F32)
    zeros_h = jnp.zeros((depth, B_HEADS), F32)
    head_params = jnp.stack([jnp.concatenate([zeros_h, a_log, lane_pad], axis=-1),
                             jnp.concatenate([zeros_h, dt_bias, lane_pad], axis=-1)], axis=1)
    g1 = norm1_g.reshape(depth, 1, d)
    g2 = norm2_g.reshape(depth, 1, d)
    ng = gdn_norm_g.reshape(depth, 1, B_HEAD_DIM)

    x2 = x.reshape(bsz * seq, d)
    for l in range(depth):
        ya, yb, gates = _token_mixing(x2, mod[l], g1, w_all, w_tail, w_conv, bias_rows, head_params, ng, l, seq)
        x2 = _mix_ffn(x2, ya, yb, gates, mod[l], g2, final_g.reshape(1, d),
                      wa, wb, wo, wi, wo_ffn, l, seq, final=(l == depth - 1))
    return x2.reshape(bsz, seq, d)
```

```python
import functools

import jax
import jax.numpy as jnp
from jax import lax
from jax.experimental import pallas as pl
from jax.experimental.pallas import tpu as pltpu

F32 = jnp.float32
BF16 = jnp.bfloat16

EPS = 1e-6
CHUNK = 64
A_HEADS = 8
A_HEAD_DIM = 64
A_WIDTH = A_HEADS * A_HEAD_DIM
A_PAST_CHUNKS = 8
A_BAND = (A_PAST_CHUNKS + 1) * CHUNK
A_MAX_REL = 128
B_HEADS = 4
B_HEAD_DIM = 128
B_WIDTH = B_HEADS * B_HEAD_DIM
CONV_K = 4
N_MOD = 6
LANES = 128
SUBLANES = 8
VMEM_LIMIT = 60 * 1024 * 1024

ROW_TILE = A_PAST_CHUNKS * CHUNK
COL_TILE = 256
FFN_CHUNK = 256
ATTN_PAIRS_PER_PHASE = 4
BIAS_ROW = 5 * LANES


def _params(*sem):
    return pltpu.CompilerParams(dimension_semantics=sem, vmem_limit_bytes=VMEM_LIMIT)


def _resident(block_shape, index_map):
    return pl.BlockSpec(block_shape, index_map, pipeline_mode=pl.Buffered(1))


def _mm(a, b):
    return jnp.dot(a.astype(BF16), b.astype(BF16), preferred_element_type=F32)


def _mm_nt(a, b):
    return lax.dot_general(a.astype(BF16), b.astype(BF16), (((1,), (1,)), ((), ())),
                           preferred_element_type=F32)


def _rows(x, g, n=CHUNK):
    return x[g * n:(g + 1) * n]


def _bmm(a, b, groups, rows_a=CHUNK, rows_b=CHUNK):
    return jnp.concatenate([_mm(_rows(a, g, rows_a), _rows(b, g, rows_b)) for g in range(groups)], axis=0)


def _sigmoid(x):
    return 1.0 / (1.0 + jnp.exp(-x))


def _silu(x):
    return x * _sigmoid(x)


def _norm_mod(x, gain, shift, scale):
    ms = jnp.mean(x * x, axis=-1, keepdims=True)
    return x * lax.rsqrt(ms + EPS) * gain * (1.0 + scale) + shift


def _lane_bcast(x, lane, width):
    return jnp.broadcast_to(x[:, lane:lane + 1], (x.shape[0], width))


def _interleave(*lists):
    tagged = sorted(((i + 0.5) / len(l), n, i) for n, l in enumerate(lists) for i in range(len(l)))
    return [lists[n][i] for _, n, i in tagged]


def _mod_kernel(c_ref, w_ref, b_ref, o_ref):
    cond = _silu(c_ref[...])
    o_ref[...] = _mm(cond, w_ref[...]) + b_ref[...]


def _modulation(c, w_ada, b_ada):
    depth, d, n = w_ada.shape
    bsz = c.shape[0]
    rows = -(-bsz // SUBLANES) * SUBLANES
    c_pad = jnp.zeros((rows, d), F32).at[:bsz].set(c)
    tn = n // 4
    out = pl.pallas_call(
        _mod_kernel,
        grid=(depth, n // tn),
        in_specs=[
            pl.BlockSpec((rows, d), lambda l, j: (0, 0)),
            pl.BlockSpec((None, d, tn), lambda l, j: (l, 0, j)),
            pl.BlockSpec((None, 1, tn), lambda l, j: (l, 0, j)),
        ],
        out_specs=pl.BlockSpec((None, rows, tn), lambda l, j: (l, 0, j)),
        out_shape=jax.ShapeDtypeStruct((depth, rows, n), F32),
        compiler_params=_params("arbitrary", "arbitrary"),
        name="adaln_modulation",
    )(c_pad, w_ada, b_ada.reshape(depth, 1, n))
    return out[:, :bsz].reshape(depth, bsz, N_MOD, d)


def _tail_kernel(a_ref, b_ref, wg_ref, wba_ref):
    n_ba = 2 * B_HEADS
    wa = a_ref.shape[1]
    wba_ref[...] = a_ref[:, 0:LANES].astype(BF16)
    gates = jnp.concatenate([a_ref[:, n_ba:wa], b_ref[:, 0:wg_ref.shape[1] - (wa - n_ba)]], axis=1)
    wg_ref[...] = gates.astype(BF16)


def _tail_weights(w_in, n_main, d):
    depth, _, n_all = w_in.shape
    win = n_main // 2
    n_gates = 2 * d
    assert win % LANES == 0 and 2 * win >= 2 * B_HEADS + n_gates and n_all == n_main + 2 * B_HEADS + n_gates
    return pl.pallas_call(
        _tail_kernel,
        grid=(depth,),
        in_specs=[pl.BlockSpec((None, d, win), lambda l: (l, 0, n_main // win)),
                  pl.BlockSpec((None, d, win), lambda l: (l, 0, n_main // win + 1))],
        out_specs=[pl.BlockSpec((None, d, n_gates), lambda l: (l, 0, 0)),
                   pl.BlockSpec((None, d, LANES), lambda l: (l, 0, 0))],
        out_shape=[jax.ShapeDtypeStruct((depth, d, n_gates), BF16), jax.ShapeDtypeStruct((depth, d, LANES), BF16)],
        compiler_params=_params("arbitrary"),
        name="gate_weight_layout",
    )(w_in, w_in)


def _attention_jobs(q_scr, k_scr, v_scr, bias_scr, ya_ref, first):
    tm = q_scr.shape[0]
    pairs = A_WIDTH // LANES
    heads_per_pair = LANES // A_HEAD_DIM
    past = A_PAST_CHUNKS * CHUNK
    lane = lax.broadcasted_iota(jnp.int32, (CHUNK, LANES), 1)
    col = lax.broadcasted_iota(jnp.int32, (1, A_BAND), 1)
    live = {}

    def scores(r0, p0):
        base = r0 + (tm - past)
        parts = []
        for p in range(p0, p0 + ATTN_PAIRS_PER_PHASE):
            cs = slice(p * LANES, (p + 1) * LANES)
            q2 = q_scr[r0:r0 + CHUNK, cs]
            zero = jnp.zeros_like(q2)
            qs = jnp.concatenate(
                [jnp.where((lane >= hh * A_HEAD_DIM) & (lane < (hh + 1) * A_HEAD_DIM), q2, zero)
                 for hh in range(heads_per_pair)], axis=0)
            parts.append(_mm_nt(qs, k_scr[base:base + A_BAND, cs]))
        valid = jnp.logical_or(col + base >= tm, jnp.logical_not(first))
        bias = bias_scr[p0 * LANES:(p0 + ATTN_PAIRS_PER_PHASE) * LANES, :]
        live["s"] = jnp.concatenate(parts, axis=0) + bias + jnp.where(valid, 0.0, -1e30)

    def softmax(r0, p0):
        s = live.pop("s")
        e = jnp.exp(s - jnp.max(s, axis=-1, keepdims=True))
        live["p"] = (e.astype(BF16), 1.0 / jnp.sum(e, axis=-1, keepdims=True))

    def output(r0, p0):
        pb, rinv = live.pop("p")
        base = r0 + (tm - past)
        for n, p in enumerate(range(p0, p0 + ATTN_PAIRS_PER_PHASE)):
            cs = slice(p * LANES, (p + 1) * LANES)
            rs = slice(n * LANES, (n + 1) * LANES)
            o = _mm(pb[rs], v_scr[base:base + A_BAND, cs]) * rinv[rs]
            o2 = jnp.where(lane < A_HEAD_DIM, o[:CHUNK], o[CHUNK:])
            ya_ref[r0:r0 + CHUNK, cs] = o2.astype(ya_ref.dtype)

    return [functools.partial(f, r0, p0) for r0 in range(0, tm, CHUNK)
            for p0 in range(0, pairs, ATTN_PAIRS_PER_PHASE) for f in (scores, softmax, output)]


def _gdn_jobs(x_ref, z_ref, ba_ref, hp_ref, ng_ref, o_ref,
              u_scr, w_scr, qd_scr, qk_scr, kdt_scr, gl_scr, o_scr, state):
    rows = x_ref.shape[0]
    groups = rows // CHUNK
    heads = range(B_HEADS)
    v_ = {}

    def pair_mask(b):
        ri, ci = v_["ri"], v_["ci"]
        return ((ri ^ ci) < 2 * b) & ((ri & b) != 0) & ((ci & b) == 0)

    def setup():
        ri = lax.broadcasted_iota(jnp.int32, (rows, CHUNK), 0) & (CHUNK - 1)
        ci = lax.broadcasted_iota(jnp.int32, (rows, CHUNK), 1)
        r1 = lax.broadcasted_iota(jnp.int32, (CHUNK, CHUNK), 0)
        c1 = lax.broadcasted_iota(jnp.int32, (CHUNK, CHUNK), 1)
        tril = jnp.where(r1 >= c1, 1.0, 0.0).astype(F32)
        v_.update(ri=ri, ci=ci, ts=[], ps=[], rhss=[])
        ba = ba_ref[...]
        sp = ba + hp_ref[1:2, :]
        softplus = jnp.maximum(sp, 0.0) + jnp.log(1.0 + jnp.exp(-jnp.abs(sp)))
        g_all = -jnp.exp(hp_ref[0:1, :]) * softplus
        gc_all = jnp.concatenate(
            [jnp.dot(tril, _rows(g_all, g), preferred_element_type=F32, precision=lax.Precision.HIGHEST)
             for g in range(groups)], axis=0)
        gc_last = jnp.concatenate(
            [jnp.broadcast_to(gc_all[(g + 1) * CHUNK - 1:(g + 1) * CHUNK, :], (CHUNK, LANES))
             for g in range(groups)], axis=0)
        v_.update(beta_all=_sigmoid(ba), g_all=g_all, gc_all=gc_all, e_gc=jnp.exp(gc_all),
                  e_rev=jnp.exp(gc_last - gc_all), e_last=jnp.exp(gc_last))

    def prep(h):
        ri, ci = v_["ri"], v_["ci"]
        q = x_ref[:, h * B_HEAD_DIM:(h + 1) * B_HEAD_DIM]
        k = x_ref[:, B_WIDTH + h * B_HEAD_DIM:B_WIDTH + (h + 1) * B_HEAD_DIM]
        v = x_ref[:, 2 * B_WIDTH + h * B_HEAD_DIM:2 * B_WIDTH + (h + 1) * B_HEAD_DIM]
        q = q * (lax.rsqrt(jnp.sum(q * q, axis=-1, keepdims=True) + EPS) * (B_HEAD_DIM ** -0.5))
        k = k * lax.rsqrt(jnp.sum(k * k, axis=-1, keepdims=True) + EPS)
        beta = _lane_bcast(v_["beta_all"], h, B_HEAD_DIM)
        egc = _lane_bcast(v_["e_gc"], B_HEADS + h, B_HEAD_DIM)
        gmat = _lane_bcast(v_["g_all"], B_HEADS + h, CHUNK)
        gc_col = _lane_bcast(v_["gc_all"], B_HEADS + h, CHUNK)
        gc_row = jnp.sum(jnp.where(ri <= ci, gmat, 0.0).reshape(groups, CHUNK, CHUNK), axis=1, keepdims=True)
        gc_row = jnp.broadcast_to(gc_row, (groups, CHUNK, CHUNK)).reshape(rows, CHUNK)
        decay = jnp.exp(jnp.where(ri >= ci, gc_col - gc_row, -jnp.inf))
        kb, qb = k.astype(BF16), q.astype(BF16)
        kk = jnp.concatenate([_mm_nt(_rows(kb, g), _rows(kb, g)) for g in range(groups)], axis=0)
        qk = jnp.concatenate([_mm_nt(_rows(qb, g), _rows(kb, g)) for g in range(groups)], axis=0)
        a = jnp.where(ri > ci, beta[:, :CHUNK] * kk * decay, 0.0)
        eye = jnp.where(ri == ci, 1.0, 0.0).astype(F32)
        v_["ts"].append(eye - jnp.where(pair_mask(1), a, 0.0))
        v_["ps"].append(a)
        v_["rhss"].append(jnp.concatenate([v * beta, k * (beta * egc)], axis=-1))
        k_dec = k * _lane_bcast(v_["e_rev"], B_HEADS + h, B_HEAD_DIM)
        qd_scr[h] = (q * egc).astype(BF16)
        qk_scr[h] = (qk * decay).astype(BF16)
        kdt_scr[h] = jnp.concatenate([_rows(k_dec, g).T for g in range(groups)], axis=0).astype(BF16)
        e_last = _lane_bcast(v_["e_last"], B_HEADS + h, B_HEAD_DIM)
        gl_scr[h] = jnp.concatenate([e_last[g * CHUNK:g * CHUNK + SUBLANES] for g in range(groups)], axis=0)

    def level(b):
        below = pair_mask(b)
        ns = [jnp.where(below, a, 0.0) for a in v_["ps"]]
        ws = [_bmm(n, t, groups) for n, t in zip(ns, v_["ts"])]
        v_["ts"] = [t - _bmm(t, w, groups) for t, w in zip(v_["ts"], ws)]

    def solve(h):
        sol = _bmm(v_["ts"][h], v_["rhss"][h], groups)
        u_scr[h] = sol[:, :B_HEAD_DIM]
        w_scr[h] = sol[:, B_HEAD_DIM:].astype(BF16)

    def step(c):
        blk = lambda ref, h, n=CHUNK: ref[h, c * n:(c + 1) * n, :]
        s_old = [state[h] for h in heads]
        r1s = [_mm(jnp.concatenate([blk(w_scr, h), blk(qd_scr, h)], axis=0), s) for h, s in zip(heads, s_old)]
        us = [blk(u_scr, h) - r[:CHUNK] for h, r in zip(heads, r1s)]
        r2s = [_mm(jnp.concatenate([blk(qk_scr, h), blk(kdt_scr, h, B_HEAD_DIM)], axis=0), u)
               for h, u in zip(heads, us)]
        for h, s, ra, rb in zip(heads, s_old, r1s, r2s):
            gl = blk(gl_scr, h, SUBLANES)
            state[h] = s * jnp.concatenate([gl] * (B_HEAD_DIM // SUBLANES), axis=0) + rb[CHUNK:]
            o_scr[c * CHUNK:(c + 1) * CHUNK, h * B_HEAD_DIM:(h + 1) * B_HEAD_DIM] = ra[CHUNK:] + rb[:CHUNK]

    def finish(h):
        cs = slice(h * B_HEAD_DIM, (h + 1) * B_HEAD_DIM)
        o = o_scr[:, cs]
        o = o * lax.rsqrt(jnp.mean(o * o, axis=-1, keepdims=True) + EPS) * ng_ref[...] * _silu(z_ref[:, cs])
        o_ref[:, cs] = o.astype(o_ref.dtype)

    levels = []
    b = 2
    while b < CHUNK:
        levels.append(functools.partial(level, b))
        b *= 2
    return ([setup] + [functools.partial(prep, h) for h in heads] + levels
            + [functools.partial(solve, h) for h in heads] + [functools.partial(step, c) for c in range(groups)]
            + [functools.partial(finish, h) for h in heads])


def _mixing_kernel(x_ref, mod_ref, g_ref, wm_ref, wg_ref, wba_ref, wc_ref, brow_ref, hp_ref, ng_ref,
                   ya_ref, yb_ref, og_ref,
                   xext, q_scr, k_scr, v_scr, bias_scr, qkv_scr, z_scr, ba_scr,
                   u_scr, w_scr, qd_scr, qk_scr, kdt_scr, gl_scr, o_scr, state,
                   *, tiles_per_batch):
    tm = x_ref.shape[0]
    a3, b3 = 3 * A_WIDTH, 3 * B_WIDTH
    first = pl.program_id(0) % tiles_per_batch == 0

    @pl.when(pl.program_id(0) == 0)
    def _():
        for hd in range(A_HEADS):
            tiled = jnp.broadcast_to(brow_ref[hd:hd + 1, :], (CHUNK, BIAS_ROW))
            rolled = pltpu.roll(tiled, BIAS_ROW - (CHUNK - 1), 1, stride=1, stride_axis=0)
            bias_scr[hd * CHUNK:(hd + 1) * CHUNK, :] = rolled[:, :A_BAND]
        k_scr[...] = jnp.zeros_like(k_scr)
        v_scr[...] = jnp.zeros_like(v_scr)

    @pl.when(first)
    def _():
        xext[0:SUBLANES, :] = jnp.zeros((SUBLANES, b3), F32)
        state[...] = jnp.zeros_like(state)

    k_scr[0:tm, :] = k_scr[tm:2 * tm, :]
    v_scr[0:tm, :] = v_scr[tm:2 * tm, :]

    h = _norm_mod(x_ref[...], g_ref[...], mod_ref[0:1, :], mod_ref[1:2, :]).astype(BF16)

    def proj(w_ref, c0, cw):
        return jnp.dot(h, w_ref[:, c0:c0 + cw], preferred_element_type=F32)

    scale = A_HEAD_DIM ** -0.5
    for c0 in range(0, A_WIDTH, COL_TILE):
        q_scr[:, c0:c0 + COL_TILE] = proj(wm_ref, c0, COL_TILE).astype(BF16) * scale
        k_scr[tm:2 * tm, c0:c0 + COL_TILE] = proj(wm_ref, A_WIDTH + c0, COL_TILE).astype(BF16)
        v_scr[tm:2 * tm, c0:c0 + COL_TILE] = proj(wm_ref, 2 * A_WIDTH + c0, COL_TILE).astype(BF16)

    def conv_tile(c0):
        full = xext[:, c0:c0 + LANES]
        acc = wc_ref[CONV_K - 1:CONV_K, c0:c0 + LANES] * full[SUBLANES:]
        for s in range(1, CONV_K):
            shifted = pltpu.roll(full, s, 0)[SUBLANES:]
            acc = acc + wc_ref[CONV_K - 1 - s:CONV_K - s, c0:c0 + LANES] * shifted
        qkv_scr[:, c0:c0 + LANES] = _silu(acc)

    def raw_job(r0):
        xext[SUBLANES:SUBLANES + tm, r0:r0 + COL_TILE] = proj(wm_ref, a3 + r0, COL_TILE)
        for c0 in range(r0, r0 + COL_TILE, LANES):
            conv_tile(c0)

    def plain_job(o_ref, w_ref, wc0, oc0):
        cw = min(COL_TILE, o_ref.shape[1])
        o_ref[:, oc0:oc0 + cw] = proj(w_ref, wc0, cw).astype(o_ref.dtype)

    attn = _attention_jobs(q_scr, k_scr, v_scr, bias_scr, ya_ref, first)
    gdn = _gdn_jobs(qkv_scr, z_scr, ba_scr, hp_ref, ng_ref, yb_ref,
                    u_scr, w_scr, qd_scr, qk_scr, kdt_scr, gl_scr, o_scr, state)
    feed = [functools.partial(raw_job, r0) for r0 in range(0, b3, COL_TILE)]
    feed += [functools.partial(plain_job, z_scr, wm_ref, a3 + b3 + c0, c0) for c0 in range(0, B_WIDTH, COL_TILE)]
    feed += [functools.partial(plain_job, ba_scr, wba_ref, 0, 0)]
    gates = [functools.partial(plain_job, og_ref, wg_ref, c0, c0) for c0 in range(0, og_ref.shape[1], COL_TILE)]
    n_early = len(attn) * len(feed) // (len(feed) + len(gates))
    for job in _interleave(feed, attn[:n_early]) + _interleave(gates, gdn, attn[n_early:]):
        job()
    xext[0:SUBLANES, :] = xext[tm:tm + SUBLANES, :]


def _token_mixing(x2, mod_l, gain, w_all, w_gates, w_ba, w_conv, bias_rows, head_params, norm_g, layer,
                  rows_per_batch):
    m, d = x2.shape
    tm = ROW_TILE
    tpb = rows_per_batch // tm
    groups = tm // CHUNK
    widths = (A_WIDTH, B_WIDTH, 2 * d)
    dtypes = (BF16, BF16, BF16)
    row_spec = lambda w: pl.BlockSpec((tm, w), lambda i: (i, 0))
    head_scr = lambda r, w, t: pltpu.VMEM((B_HEADS, r, w), t)
    return pl.pallas_call(
        functools.partial(_mixing_kernel, tiles_per_batch=tpb),
        grid=(m // tm,),
        in_specs=[
            row_spec(d),
            pl.BlockSpec((None, N_MOD, d), lambda i: (i // tpb, 0, 0)),
            _resident((None, 1, d), lambda i: (layer, 0, 0)),
            _resident((None, d, 3 * A_WIDTH + 4 * B_WIDTH), lambda i: (layer, 0, 0)),
            _resident((None, d, 2 * d), lambda i: (layer, 0, 0)),
            _resident((None, d, LANES), lambda i: (layer, 0, 0)),
            _resident((None, CONV_K, 3 * B_WIDTH), lambda i: (layer, 0, 0)),
            _resident((None, A_HEADS, BIAS_ROW), lambda i: (layer, 0, 0)),
            _resident((None, 2, LANES), lambda i: (layer, 0, 0)),
            _resident((None, 1, B_HEAD_DIM), lambda i: (layer, 0, 0)),
        ],
        out_specs=[row_spec(w) for w in widths],
        out_shape=[jax.ShapeDtypeStruct((m, w), t) for w, t in zip(widths, dtypes)],
        scratch_shapes=[pltpu.VMEM((tm + SUBLANES, 3 * B_WIDTH), F32),
                        pltpu.VMEM((tm, A_WIDTH), BF16),
                        pltpu.VMEM((2 * tm, A_WIDTH), BF16), pltpu.VMEM((2 * tm, A_WIDTH), BF16),
                        pltpu.VMEM((A_HEADS * CHUNK, A_BAND), F32),
                        pltpu.VMEM((tm, 3 * B_WIDTH), F32), pltpu.VMEM((tm, B_WIDTH), F32),
                        pltpu.VMEM((tm, LANES), F32),
                        head_scr(tm, B_HEAD_DIM, F32), head_scr(tm, B_HEAD_DIM, BF16),
                        head_scr(tm, B_HEAD_DIM, BF16), head_scr(tm, CHUNK, BF16),
                        head_scr(groups * B_HEAD_DIM, CHUNK, BF16), head_scr(groups * SUBLANES, B_HEAD_DIM, F32),
                        pltpu.VMEM((tm, B_WIDTH), F32),
                        pltpu.VMEM((B_HEADS, B_HEAD_DIM, B_HEAD_DIM), F32)],
        compiler_params=_params("arbitrary"),
        name="token_mixing",
    )(x2, mod_l, gain, w_all, w_gates, w_ba, w_conv, bias_rows, head_params, norm_g)


def _mix_ffn_kernel(x_ref, ya_ref, yb_ref, gt_ref, mod_ref, g_ref, fg_ref, wa_ref, wb_ref, wm_ref, wi_ref, wo_ref,
                    o_ref, x1_ref, acc_ref, *, final):
    d = x_ref.shape[-1]
    gate_a = _sigmoid(gt_ref[:, 0:d].astype(F32))
    gate_b = _sigmoid(gt_ref[:, d:2 * d].astype(F32))
    merged = (gate_a * jnp.dot(ya_ref[...], wa_ref[...], preferred_element_type=F32)
              + gate_b * jnp.dot(yb_ref[...], wb_ref[...], preferred_element_type=F32))
    x1_ref[...] = x_ref[...] + mod_ref[2:3, :] * _mm(merged, wm_ref[...])
    h = _norm_mod(x1_ref[...], g_ref[...], mod_ref[3:4, :], mod_ref[4:5, :]).astype(BF16)
    hidden = wo_ref.shape[0]
    for j in range(hidden // FFN_CHUNK):
        js = slice(j * FFN_CHUNK, (j + 1) * FFN_CHUNK)
        gate = jnp.dot(h, wi_ref[:, js], preferred_element_type=F32)
        up = jnp.dot(h, wi_ref[:, hidden + j * FFN_CHUNK:hidden + (j + 1) * FFN_CHUNK],
                     preferred_element_type=F32)
        act = (_silu(gate) * up).astype(BF16)
        part = jnp.dot(act, wo_ref[js, :], preferred_element_type=F32)
        if j == 0:
            acc_ref[...] = part
        else:
            acc_ref[...] += part
    out = x1_ref[...] + mod_ref[5:6, :] * acc_ref[...]
    if final:
        out = out * lax.rsqrt(jnp.mean(out * out, axis=-1, keepdims=True) + EPS) * fg_ref[...]
    o_ref[...] = out


def _mix_ffn(x2, ya, yb, gates, mod_l, gain, final_g, wa, wb, wm, wi, wo, layer, rows_per_batch, final):
    m, d = x2.shape
    tm = ROW_TILE
    tpb = rows_per_batch // tm
    hidden = wo.shape[1]
    row_spec = lambda w: pl.BlockSpec((tm, w), lambda i: (i, 0))
    return pl.pallas_call(
        functools.partial(_mix_ffn_kernel, final=final),
        grid=(m // tm,),
        in_specs=[row_spec(d), row_spec(A_WIDTH), row_spec(B_WIDTH), row_spec(2 * d),
                  pl.BlockSpec((None, N_MOD, d), lambda i: (i // tpb, 0, 0)),
                  _resident((None, 1, d), lambda i: (layer, 0, 0)),
                  _resident((1, d), lambda i: (0, 0)),
                  _resident((None, A_WIDTH, d), lambda i: (layer, 0, 0)),
                  _resident((None, B_WIDTH, d), lambda i: (layer, 0, 0)),
                  _resident((None, d, d), lambda i: (layer, 0, 0)),
                  _resident((None, d, 2 * hidden), lambda i: (layer, 0, 0)),
                  _resident((None, hidden, d), lambda i: (layer, 0, 0))],
        out_specs=row_spec(d),
        out_shape=jax.ShapeDtypeStruct((m, d), F32),
        scratch_shapes=[pltpu.VMEM((tm, d), F32), pltpu.VMEM((tm, d), F32)],
        compiler_params=_params("arbitrary"),
        name="merge_swiglu",
    )(x2, ya, yb, gates, mod_l, gain, final_g, wa, wb, wm, wi, wo)


def _bias_rows(rel_table):
    t = rel_table.astype(F32)
    lo = A_MAX_REL - (CHUNK - 1)
    ramp = t[..., lo:][..., ::-1]
    n_far = CHUNK + A_BAND - 1 - ramp.shape[-1]
    far = jnp.broadcast_to(t[..., -1:], t.shape[:-1] + (n_far,))
    pad = jnp.zeros(t.shape[:-1] + (BIAS_ROW - (CHUNK + A_BAND - 1),), F32)
    return jnp.concatenate([far, ramp, pad], axis=-1)


def kernel(x, c, w_ada, b_ada, norm1_g, norm2_g, w_in, rel_table, w_conv, a_log, dt_bias, gdn_norm_g,
           w_branch_a, w_branch_b, w_out, w_ffn_in, w_ffn_out, final_g):
    bsz, seq, d = x.shape
    depth = w_ada.shape[0]
    assert seq % ROW_TILE == 0 and d % LANES == 0

    mod = _modulation(c, w_ada, b_ada)
    n_main = 3 * A_WIDTH + 4 * B_WIDTH
    w_all = w_in.astype(BF16)
    w_gates, w_ba = _tail_weights(w_in, n_main, d)
    wi, wo_ffn = w_ffn_in.astype(BF16), w_ffn_out.astype(BF16)
    wa, wb, wo = w_branch_a.astype(BF16), w_branch_b.astype(BF16), w_out.astype(BF16)
    bias_rows = _bias_rows(rel_table)
    lane_pad = jnp.zeros((depth, LANES - 2 * B_HEADS), F32)
    zeros_h = jnp.zeros((depth, B_HEADS), F32)
    head_params = jnp.stack([jnp.concatenate([zeros_h, a_log, lane_pad], axis=-1),
                             jnp.concatenate([zeros_h, dt_bias, lane_pad], axis=-1)], axis=1)
    g1 = norm1_g.reshape(depth, 1, d)
    g2 = norm2_g.reshape(depth, 1, d)
    ng = gdn_norm_g.reshape(depth, 1, B_HEAD_DIM)

    x2 = x.reshape(bsz * seq, d)
    for l in range(depth):
        ya, yb, gates = _token_mixing(x2, mod[l], g1, w_all, w_gates, w_ba, w_conv, bias_rows, head_params, ng, l,
                                      seq)
        x2 = _mix_ffn(x2, ya, yb, gates, mod[l], g2, final_g.reshape(1, d),
                      wa, wb, wo, wi, wo_ffn, l, seq, final=(l == depth - 1))
    return x2.reshape(bsz, seq, d)
```

```python
import functools

import jax
import jax.numpy as jnp
from jax import lax
from jax.experimental import pallas as pl
from jax.experimental.pallas import tpu as pltpu

F32 = jnp.float32
BF16 = jnp.bfloat16

EPS = 1e-6
CHUNK = 64
A_HEADS = 8
A_HEAD_DIM = 64
A_WIDTH = A_HEADS * A_HEAD_DIM
A_PAST_CHUNKS = 8
A_BAND = (A_PAST_CHUNKS + 1) * CHUNK
A_MAX_REL = 128
B_HEADS = 4
B_HEAD_DIM = 128
B_WIDTH = B_HEADS * B_HEAD_DIM
CONV_K = 4
N_MOD = 6
LANES = 128
SUBLANES = 8
VMEM_LIMIT = 60 * 1024 * 1024

ROW_TILE = A_PAST_CHUNKS * CHUNK
COL_TILE = 256
FFN_CHUNK = 256
ATTN_PAIRS_PER_PHASE = 4
BIAS_ROW = 5 * LANES


def _params(*sem):
    return pltpu.CompilerParams(dimension_semantics=sem, vmem_limit_bytes=VMEM_LIMIT)


def _resident(block_shape, index_map):
    return pl.BlockSpec(block_shape, index_map, pipeline_mode=pl.Buffered(1))


def _mm(a, b):
    return jnp.dot(a.astype(BF16), b.astype(BF16), preferred_element_type=F32)


def _mm_nt(a, b):
    return lax.dot_general(a.astype(BF16), b.astype(BF16), (((1,), (1,)), ((), ())),
                           preferred_element_type=F32)


def _rows(x, g, n=CHUNK):
    return x[g * n:(g + 1) * n]


def _bmm(a, b, groups, rows_a=CHUNK, rows_b=CHUNK):
    return jnp.concatenate([_mm(_rows(a, g, rows_a), _rows(b, g, rows_b)) for g in range(groups)], axis=0)


def _sigmoid(x):
    return 1.0 / (1.0 + jnp.exp(-x))


def _silu(x):
    return x * _sigmoid(x)


def _norm_mod(x, gain, shift, scale):
    ms = jnp.mean(x * x, axis=-1, keepdims=True)
    return x * lax.rsqrt(ms + EPS) * gain * (1.0 + scale) + shift


def _lane_bcast(x, lane, width):
    return jnp.broadcast_to(x[:, lane:lane + 1], (x.shape[0], width))


def _interleave(*lists):
    tagged = sorted(((i + 0.5) / len(l), n, i) for n, l in enumerate(lists) for i in range(len(l)))
    return [lists[n][i] for _, n, i in tagged]


def _mod_kernel(c_ref, w_ref, b_ref, o_ref):
    cond = _silu(c_ref[...])
    o_ref[...] = _mm(cond, w_ref[...]) + b_ref[...]


def _modulation(c, w_ada, b_ada):
    depth, d, n = w_ada.shape
    bsz = c.shape[0]
    rows = -(-bsz // SUBLANES) * SUBLANES
    c_pad = jnp.zeros((rows, d), F32).at[:bsz].set(c)
    tn = n // 4
    out = pl.pallas_call(
        _mod_kernel,
        grid=(depth, n // tn),
        in_specs=[
            pl.BlockSpec((rows, d), lambda l, j: (0, 0)),
            pl.BlockSpec((None, d, tn), lambda l, j: (l, 0, j)),
            pl.BlockSpec((None, 1, tn), lambda l, j: (l, 0, j)),
        ],
        out_specs=pl.BlockSpec((None, rows, tn), lambda l, j: (l, 0, j)),
        out_shape=jax.ShapeDtypeStruct((depth, rows, n), F32),
        compiler_params=_params("arbitrary", "arbitrary"),
        name="adaln_modulation",
    )(c_pad, w_ada, b_ada.reshape(depth, 1, n))
    return out[:, :bsz].reshape(depth, bsz, N_MOD, d)


def _tail_kernel(a_ref, b_ref, wg_ref, wba_ref):
    n_ba = 2 * B_HEADS
    wa = a_ref.shape[1]
    wba_ref[...] = a_ref[:, 0:LANES]
    wg_ref[...] = jnp.concatenate([a_ref[:, n_ba:wa], b_ref[:, 0:wg_ref.shape[1] - (wa - n_ba)]], axis=1)


def _tail_weights(w_all, n_main, d):
    depth, _, n_all = w_all.shape
    win = n_main // 2
    n_gates = 2 * d
    assert win % LANES == 0 and 2 * win >= 2 * B_HEADS + n_gates and n_all == n_main + 2 * B_HEADS + n_gates
    return pl.pallas_call(
        _tail_kernel,
        grid=(depth,),
        in_specs=[pl.BlockSpec((None, d, win), lambda l: (l, 0, n_main // win)),
                  pl.BlockSpec((None, d, win), lambda l: (l, 0, n_main // win + 1))],
        out_specs=[pl.BlockSpec((None, d, n_gates), lambda l: (l, 0, 0)),
                   pl.BlockSpec((None, d, LANES), lambda l: (l, 0, 0))],
        out_shape=[jax.ShapeDtypeStruct((depth, d, n_gates), BF16), jax.ShapeDtypeStruct((depth, d, LANES), BF16)],
        compiler_params=_params("arbitrary"),
        name="gate_weight_layout",
    )(w_all, w_all)


def _attention_jobs(q_scr, k_scr, v_scr, bias_scr, ya_ref, first):
    tm = q_scr.shape[0]
    pairs = A_WIDTH // LANES
    heads_per_pair = LANES // A_HEAD_DIM
    past = A_PAST_CHUNKS * CHUNK
    lane = lax.broadcasted_iota(jnp.int32, (CHUNK, LANES), 1)
    col = lax.broadcasted_iota(jnp.int32, (1, A_BAND), 1)
    live = {}

    def scores(r0, p0):
        base = r0 + (tm - past)
        parts = []
        for p in range(p0, p0 + ATTN_PAIRS_PER_PHASE):
            cs = slice(p * LANES, (p + 1) * LANES)
            q2 = q_scr[r0:r0 + CHUNK, cs]
            zero = jnp.zeros_like(q2)
            qs = jnp.concatenate(
                [jnp.where((lane >= hh * A_HEAD_DIM) & (lane < (hh + 1) * A_HEAD_DIM), q2, zero)
                 for hh in range(heads_per_pair)], axis=0)
            parts.append(_mm_nt(qs, k_scr[base:base + A_BAND, cs]))
        valid = jnp.logical_or(col + base >= tm, jnp.logical_not(first))
        bias = bias_scr[p0 * LANES:(p0 + ATTN_PAIRS_PER_PHASE) * LANES, :]
        live["s"] = jnp.concatenate(parts, axis=0) + bias + jnp.where(valid, 0.0, -1e30)

    def softmax(r0, p0):
        s = live.pop("s")
        e = jnp.exp(s - jnp.max(s, axis=-1, keepdims=True))
        live["p"] = (e.astype(BF16), 1.0 / jnp.sum(e, axis=-1, keepdims=True))

    def output(r0, p0):
        pb, rinv = live.pop("p")
        base = r0 + (tm - past)
        for n, p in enumerate(range(p0, p0 + ATTN_PAIRS_PER_PHASE)):
            cs = slice(p * LANES, (p + 1) * LANES)
            rs = slice(n * LANES, (n + 1) * LANES)
            o = _mm(pb[rs], v_scr[base:base + A_BAND, cs]) * rinv[rs]
            o2 = jnp.where(lane < A_HEAD_DIM, o[:CHUNK], o[CHUNK:])
            ya_ref[r0:r0 + CHUNK, cs] = o2.astype(ya_ref.dtype)

    return [functools.partial(f, r0, p0) for r0 in range(0, tm, CHUNK)
            for p0 in range(0, pairs, ATTN_PAIRS_PER_PHASE) for f in (scores, softmax, output)]


def _gdn_jobs(x_ref, z_ref, ba_ref, hp_ref, ng_ref, o_ref,
              u_scr, w_scr, qd_scr, qk_scr, kdt_scr, gl_scr, o_scr, state):
    rows = x_ref.shape[0]
    groups = rows // CHUNK
    heads = range(B_HEADS)
    v_ = {}

    def pair_mask(b):
        ri, ci = v_["ri"], v_["ci"]
        return ((ri ^ ci) < 2 * b) & ((ri & b) != 0) & ((ci & b) == 0)

    def setup():
        ri = lax.broadcasted_iota(jnp.int32, (rows, CHUNK), 0) & (CHUNK - 1)
        ci = lax.broadcasted_iota(jnp.int32, (rows, CHUNK), 1)
        r1 = lax.broadcasted_iota(jnp.int32, (CHUNK, CHUNK), 0)
        c1 = lax.broadcasted_iota(jnp.int32, (CHUNK, CHUNK), 1)
        tril = jnp.where(r1 >= c1, 1.0, 0.0).astype(F32)
        v_.update(ri=ri, ci=ci, ts=[], ps=[], rhss=[])
        ba = ba_ref[...]
        sp = ba + hp_ref[1:2, :]
        softplus = jnp.maximum(sp, 0.0) + jnp.log(1.0 + jnp.exp(-jnp.abs(sp)))
        g_all = -jnp.exp(hp_ref[0:1, :]) * softplus
        gc_all = jnp.concatenate(
            [jnp.dot(tril, _rows(g_all, g), preferred_element_type=F32, precision=lax.Precision.HIGHEST)
             for g in range(groups)], axis=0)
        gc_last = jnp.concatenate(
            [jnp.broadcast_to(gc_all[(g + 1) * CHUNK - 1:(g + 1) * CHUNK, :], (CHUNK, LANES))
             for g in range(groups)], axis=0)
        v_.update(beta_all=_sigmoid(ba), g_all=g_all, gc_all=gc_all, e_gc=jnp.exp(gc_all),
                  e_rev=jnp.exp(gc_last - gc_all), e_last=jnp.exp(gc_last))

    def prep(h):
        ri, ci = v_["ri"], v_["ci"]
        q = x_ref[:, h * B_HEAD_DIM:(h + 1) * B_HEAD_DIM]
        k = x_ref[:, B_WIDTH + h * B_HEAD_DIM:B_WIDTH + (h + 1) * B_HEAD_DIM]
        v = x_ref[:, 2 * B_WIDTH + h * B_HEAD_DIM:2 * B_WIDTH + (h + 1) * B_HEAD_DIM]
        q = q * (lax.rsqrt(jnp.sum(q * q, axis=-1, keepdims=True) + EPS) * (B_HEAD_DIM ** -0.5))
        k = k * lax.rsqrt(jnp.sum(k * k, axis=-1, keepdims=True) + EPS)
        beta = _lane_bcast(v_["beta_all"], h, B_HEAD_DIM)
        egc = _lane_bcast(v_["e_gc"], B_HEADS + h, B_HEAD_DIM)
        gmat = _lane_bcast(v_["g_all"], B_HEADS + h, CHUNK)
        gc_col = _lane_bcast(v_["gc_all"], B_HEADS + h, CHUNK)
        gc_row = jnp.sum(jnp.where(ri <= ci, gmat, 0.0).reshape(groups, CHUNK, CHUNK), axis=1, keepdims=True)
        gc_row = jnp.broadcast_to(gc_row, (groups, CHUNK, CHUNK)).reshape(rows, CHUNK)
        decay = jnp.exp(jnp.where(ri >= ci, gc_col - gc_row, -jnp.inf))
        kb, qb = k.astype(BF16), q.astype(BF16)
        kk = jnp.concatenate([_mm_nt(_rows(kb, g), _rows(kb, g)) for g in range(groups)], axis=0)
        qk = jnp.concatenate([_mm_nt(_rows(qb, g), _rows(kb, g)) for g in range(groups)], axis=0)
        a = jnp.where(ri > ci, beta[:, :CHUNK] * kk * decay, 0.0)
        eye = jnp.where(ri == ci, 1.0, 0.0).astype(F32)
        v_["ts"].append(eye - jnp.where(pair_mask(1), a, 0.0))
        v_["ps"].append(a)
        v_["rhss"].append(jnp.concatenate([v * beta, k * (beta * egc)], axis=-1))
        k_dec = k * _lane_bcast(v_["e_rev"], B_HEADS + h, B_HEAD_DIM)
        qd_scr[h] = (q * egc).astype(BF16)
        qk_scr[h] = (qk * decay).astype(BF16)
        kdt_scr[h] = jnp.concatenate([_rows(k_dec, g).T for g in range(groups)], axis=0).astype(BF16)
        e_last = _lane_bcast(v_["e_last"], B_HEADS + h, B_HEAD_DIM)
        gl_scr[h] = jnp.concatenate([e_last[g * CHUNK:g * CHUNK + SUBLANES] for g in range(groups)], axis=0)

    def level(b):
        below = pair_mask(b)
        ns = [jnp.where(below, a, 0.0) for a in v_["ps"]]
        ws = [_bmm(n, t, groups) for n, t in zip(ns, v_["ts"])]
        v_["ts"] = [t - _bmm(t, w, groups) for t, w in zip(v_["ts"], ws)]

    def solve(h):
        sol = _bmm(v_["ts"][h], v_["rhss"][h], groups)
        u_scr[h] = sol[:, :B_HEAD_DIM]
        w_scr[h] = sol[:, B_HEAD_DIM:].astype(BF16)

    def step(c):
        blk = lambda ref, h, n=CHUNK: ref[h, c * n:(c + 1) * n, :]
        s_old = [state[h] for h in heads]
        r1s = [_mm(jnp.concatenate([blk(w_scr, h), blk(qd_scr, h)], axis=0), s) for h, s in zip(heads, s_old)]
        us = [blk(u_scr, h) - r[:CHUNK] for h, r in zip(heads, r1s)]
        r2s = [_mm(jnp.concatenate([blk(qk_scr, h), blk(kdt_scr, h, B_HEAD_DIM)], axis=0), u)
               for h, u in zip(heads, us)]
        for h, s, ra, rb in zip(heads, s_old, r1s, r2s):
            gl = blk(gl_scr, h, SUBLANES)
            state[h] = s * jnp.concatenate([gl] * (B_HEAD_DIM // SUBLANES), axis=0) + rb[CHUNK:]
            o_scr[c * CHUNK:(c + 1) * CHUNK, h * B_HEAD_DIM:(h + 1) * B_HEAD_DIM] = ra[CHUNK:] + rb[:CHUNK]

    def finish(h):
        cs = slice(h * B_HEAD_DIM, (h + 1) * B_HEAD_DIM)
        o = o_scr[:, cs]
        o = o * lax.rsqrt(jnp.mean(o * o, axis=-1, keepdims=True) + EPS) * ng_ref[...] * _silu(z_ref[:, cs])
        o_ref[:, cs] = o.astype(o_ref.dtype)

    levels = []
    b = 2
    while b < CHUNK:
        levels.append(functools.partial(level, b))
        b *= 2
    return ([setup] + [functools.partial(prep, h) for h in heads] + levels
            + [functools.partial(solve, h) for h in heads] + [functools.partial(step, c) for c in range(groups)]
            + [functools.partial(finish, h) for h in heads])


def _mixing_kernel(x_ref, mod_ref, g_ref, wm_ref, wg_ref, wba_ref, wc_ref, brow_ref, hp_ref, ng_ref,
                   ya_ref, yb_ref, og_ref,
                   xext, q_scr, k_scr, v_scr, bias_scr, qkv_scr, z_scr, ba_scr,
                   u_scr, w_scr, qd_scr, qk_scr, kdt_scr, gl_scr, o_scr, state,
                   *, tiles_per_batch):
    tm = x_ref.shape[0]
    a3, b3 = 3 * A_WIDTH, 3 * B_WIDTH
    first = pl.program_id(0) % tiles_per_batch == 0

    @pl.when(pl.program_id(0) == 0)
    def _():
        for hd in range(A_HEADS):
            tiled = jnp.broadcast_to(brow_ref[hd:hd + 1, :], (CHUNK, BIAS_ROW))
            rolled = pltpu.roll(tiled, BIAS_ROW - (CHUNK - 1), 1, stride=1, stride_axis=0)
            bias_scr[hd * CHUNK:(hd + 1) * CHUNK, :] = rolled[:, :A_BAND]
        k_scr[...] = jnp.zeros_like(k_scr)
        v_scr[...] = jnp.zeros_like(v_scr)

    @pl.when(first)
    def _():
        xext[0:SUBLANES, :] = jnp.zeros((SUBLANES, b3), F32)
        state[...] = jnp.zeros_like(state)

    k_scr[0:tm, :] = k_scr[tm:2 * tm, :]
    v_scr[0:tm, :] = v_scr[tm:2 * tm, :]

    h = _norm_mod(x_ref[...], g_ref[...], mod_ref[0:1, :], mod_ref[1:2, :]).astype(BF16)

    def proj(w_ref, c0, cw):
        return jnp.dot(h, w_ref[:, c0:c0 + cw], preferred_element_type=F32)

    scale = A_HEAD_DIM ** -0.5
    for c0 in range(0, A_WIDTH, COL_TILE):
        q_scr[:, c0:c0 + COL_TILE] = proj(wm_ref, c0, COL_TILE).astype(BF16) * scale
        k_scr[tm:2 * tm, c0:c0 + COL_TILE] = proj(wm_ref, A_WIDTH + c0, COL_TILE).astype(BF16)
        v_scr[tm:2 * tm, c0:c0 + COL_TILE] = proj(wm_ref, 2 * A_WIDTH + c0, COL_TILE).astype(BF16)

    def conv_tile(c0):
        full = xext[:, c0:c0 + LANES]
        acc = wc_ref[CONV_K - 1:CONV_K, c0:c0 + LANES] * full[SUBLANES:]
        for s in range(1, CONV_K):
            shifted = pltpu.roll(full, s, 0)[SUBLANES:]
            acc = acc + wc_ref[CONV_K - 1 - s:CONV_K - s, c0:c0 + LANES] * shifted
        qkv_scr[:, c0:c0 + LANES] = _silu(acc)

    def raw_job(r0):
        xext[SUBLANES:SUBLANES + tm, r0:r0 + COL_TILE] = proj(wm_ref, a3 + r0, COL_TILE)
        for c0 in range(r0, r0 + COL_TILE, LANES):
            conv_tile(c0)

    def plain_job(o_ref, w_ref, wc0, oc0):
        cw = min(COL_TILE, o_ref.shape[1])
        o_ref[:, oc0:oc0 + cw] = proj(w_ref, wc0, cw).astype(o_ref.dtype)

    attn = _attention_jobs(q_scr, k_scr, v_scr, bias_scr, ya_ref, first)
    gdn = _gdn_jobs(qkv_scr, z_scr, ba_scr, hp_ref, ng_ref, yb_ref,
                    u_scr, w_scr, qd_scr, qk_scr, kdt_scr, gl_scr, o_scr, state)
    feed = [functools.partial(raw_job, r0) for r0 in range(0, b3, COL_TILE)]
    feed += [functools.partial(plain_job, z_scr, wm_ref, a3 + b3 + c0, c0) for c0 in range(0, B_WIDTH, COL_TILE)]
    feed += [functools.partial(plain_job, ba_scr, wba_ref, 0, 0)]
    gates = [functools.partial(plain_job, og_ref, wg_ref, c0, c0) for c0 in range(0, og_ref.shape[1], COL_TILE)]
    n_early = len(attn) * len(feed) // (len(feed) + len(gates))
    for job in _interleave(feed, attn[:n_early]) + _interleave(gates, gdn, attn[n_early:]):
        job()
    xext[0:SUBLANES, :] = xext[tm:tm + SUBLANES, :]


def _token_mixing(x2, mod_l, gain, w_all, w_gates, w_ba, w_conv, bias_rows, head_params, norm_g, layer,
                  rows_per_batch):
    m, d = x2.shape
    tm = ROW_TILE
    tpb = rows_per_batch // tm
    groups = tm // CHUNK
    widths = (A_WIDTH, B_WIDTH, 2 * d)
    dtypes = (BF16, BF16, BF16)
    row_spec = lambda w: pl.BlockSpec((tm, w), lambda i: (i, 0))
    head_scr = lambda r, w, t: pltpu.VMEM((B_HEADS, r, w), t)
    return pl.pallas_call(
        functools.partial(_mixing_kernel, tiles_per_batch=tpb),
        grid=(m // tm,),
        in_specs=[
            row_spec(d),
            pl.BlockSpec((None, N_MOD, d), lambda i: (i // tpb, 0, 0)),
            _resident((None, 1, d), lambda i: (layer, 0, 0)),
            _resident((None, d, 3 * A_WIDTH + 4 * B_WIDTH), lambda i: (layer, 0, 0)),
            _resident((None, d, 2 * d), lambda i: (layer, 0, 0)),
            _resident((None, d, LANES), lambda i: (layer, 0, 0)),
            _resident((None, CONV_K, 3 * B_WIDTH), lambda i: (layer, 0, 0)),
            _resident((None, A_HEADS, BIAS_ROW), lambda i: (layer, 0, 0)),
            _resident((None, 2, LANES), lambda i: (layer, 0, 0)),
            _resident((None, 1, B_HEAD_DIM), lambda i: (layer, 0, 0)),
        ],
        out_specs=[row_spec(w) for w in widths],
        out_shape=[jax.ShapeDtypeStruct((m, w), t) for w, t in zip(widths, dtypes)],
        scratch_shapes=[pltpu.VMEM((tm + SUBLANES, 3 * B_WIDTH), F32),
                        pltpu.VMEM((tm, A_WIDTH), BF16),
                        pltpu.VMEM((2 * tm, A_WIDTH), BF16), pltpu.VMEM((2 * tm, A_WIDTH), BF16),
                        pltpu.VMEM((A_HEADS * CHUNK, A_BAND), F32),
                        pltpu.VMEM((tm, 3 * B_WIDTH), F32), pltpu.VMEM((tm, B_WIDTH), F32),
                        pltpu.VMEM((tm, LANES), F32),
                        head_scr(tm, B_HEAD_DIM, F32), head_scr(tm, B_HEAD_DIM, BF16),
                        head_scr(tm, B_HEAD_DIM, BF16), head_scr(tm, CHUNK, BF16),
                        head_scr(groups * B_HEAD_DIM, CHUNK, BF16), head_scr(groups * SUBLANES, B_HEAD_DIM, F32),
                        pltpu.VMEM((tm, B_WIDTH), F32),
                        pltpu.VMEM((B_HEADS, B_HEAD_DIM, B_HEAD_DIM), F32)],
        compiler_params=_params("arbitrary"),
        name="token_mixing",
    )(x2, mod_l, gain, w_all, w_gates, w_ba, w_conv, bias_rows, head_params, norm_g)


def _mix_ffn_kernel(x_ref, ya_ref, yb_ref, gt_ref, mod_ref, g_ref, fg_ref, wa_ref, wb_ref, wm_ref, wi_ref, wo_ref,
                    o_ref, x1_ref, acc_ref, *, final):
    d = x_ref.shape[-1]
    gate_a = _sigmoid(gt_ref[:, 0:d].astype(F32))
    gate_b = _sigmoid(gt_ref[:, d:2 * d].astype(F32))
    merged = (gate_a * jnp.dot(ya_ref[...], wa_ref[...], preferred_element_type=F32)
              + gate_b * jnp.dot(yb_ref[...], wb_ref[...], preferred_element_type=F32))
    x1_ref[...] = x_ref[...] + mod_ref[2:3, :] * _mm(merged, wm_ref[...])
    h = _norm_mod(x1_ref[...], g_ref[...], mod_ref[3:4, :], mod_ref[4:5, :]).astype(BF16)
    hidden = wo_ref.shape[0]
    for j in range(hidden // FFN_CHUNK):
        js = slice(j * FFN_CHUNK, (j + 1) * FFN_CHUNK)
        gate = jnp.dot(h, wi_ref[:, js], preferred_element_type=F32)
        up = jnp.dot(h, wi_ref[:, hidden + j * FFN_CHUNK:hidden + (j + 1) * FFN_CHUNK],
                     preferred_element_type=F32)
        act = (_silu(gate) * up).astype(BF16)
        part = jnp.dot(act, wo_ref[js, :], preferred_element_type=F32)
        if j == 0:
            acc_ref[...] = part
        else:
            acc_ref[...] += part
    out = x1_ref[...] + mod_ref[5:6, :] * acc_ref[...]
    if final:
        out = out * lax.rsqrt(jnp.mean(out * out, axis=-1, keepdims=True) + EPS) * fg_ref[...]
    o_ref[...] = out


def _mix_ffn(x2, ya, yb, gates, mod_l, gain, final_g, wa, wb, wm, wi, wo, layer, rows_per_batch, final):
    m, d = x2.shape
    tm = ROW_TILE
    tpb = rows_per_batch // tm
    hidden = wo.shape[1]
    row_spec = lambda w: pl.BlockSpec((tm, w), lambda i: (i, 0))
    return pl.pallas_call(
        functools.partial(_mix_ffn_kernel, final=final),
        grid=(m // tm,),
        in_specs=[row_spec(d), row_spec(A_WIDTH), row_spec(B_WIDTH), row_spec(2 * d),
                  pl.BlockSpec((None, N_MOD, d), lambda i: (i // tpb, 0, 0)),
                  _resident((None, 1, d), lambda i: (layer, 0, 0)),
                  _resident((1, d), lambda i: (0, 0)),
                  _resident((None, A_WIDTH, d), lambda i: (layer, 0, 0)),
                  _resident((None, B_WIDTH, d), lambda i: (layer, 0, 0)),
                  _resident((None, d, d), lambda i: (layer, 0, 0)),
                  _resident((None, d, 2 * hidden), lambda i: (layer, 0, 0)),
                  _resident((None, hidden, d), lambda i: (layer, 0, 0))],
        out_specs=row_spec(d),
        out_shape=jax.ShapeDtypeStruct((m, d), F32),
        scratch_shapes=[pltpu.VMEM((tm, d), F32), pltpu.VMEM((tm, d), F32)],
        compiler_params=_params("arbitrary"),
        name="merge_swiglu",
    )(x2, ya, yb, gates, mod_l, gain, final_g, wa, wb, wm, wi, wo)


def _bias_rows(rel_table):
    t = rel_table.astype(F32)
    lo = A_MAX_REL - (CHUNK - 1)
    ramp = t[..., lo:][..., ::-1]
    n_far = CHUNK + A_BAND - 1 - ramp.shape[-1]
    far = jnp.broadcast_to(t[..., -1:], t.shape[:-1] + (n_far,))
    pad = jnp.zeros(t.shape[:-1] + (BIAS_ROW - (CHUNK + A_BAND - 1),), F32)
    return jnp.concatenate([far, ramp, pad], axis=-1)


def kernel(x, c, w_ada, b_ada, norm1_g, norm2_g, w_in, rel_table, w_conv, a_log, dt_bias, gdn_norm_g,
           w_branch_a, w_branch_b, w_out, w_ffn_in, w_ffn_out, final_g):
    bsz, seq, d = x.shape
    depth = w_ada.shape[0]
    assert seq % ROW_TILE == 0 and d % LANES == 0

    mod = _modulation(c, w_ada, b_ada)
    n_main = 3 * A_WIDTH + 4 * B_WIDTH
    w_all = w_in.astype(BF16)
    w_gates, w_ba = _tail_weights(w_all, n_main, d)
    wi, wo_ffn = w_ffn_in.astype(BF16), w_ffn_out.astype(BF16)
    wa, wb, wo = w_branch_a.astype(BF16), w_branch_b.astype(BF16), w_out.astype(BF16)
    bias_rows = _bias_rows(rel_table)
    lane_pad = jnp.zeros((depth, LANES - 2 * B_HEADS), F32)
    zeros_h = jnp.zeros((depth, B_HEADS), F32)
    head_params = jnp.stack([jnp.concatenate([zeros_h, a_log, lane_pad], axis=-1),
                             jnp.concatenate([zeros_h, dt_bias, lane_pad], axis=-1)], axis=1)
    g1 = norm1_g.reshape(depth, 1, d)
    g2 = norm2_g.reshape(depth, 1, d)
    ng = gdn_norm_g.reshape(depth, 1, B_HEAD_DIM)

    x2 = x.reshape(bsz * seq, d)
    for l in range(depth):
        ya, yb, gates = _token_mixing(x2, mod[l], g1, w_all, w_gates, w_ba, w_conv, bias_rows, head_params, ng, l,
                                      seq)
        x2 = _mix_ffn(x2, ya, yb, gates, mod[l], g2, final_g.reshape(1, d),
                      wa, wb, wo, wi, wo_ffn, l, seq, final=(l == depth - 1))
    return x2.reshape(bsz, seq, d)
```

```python
import functools

import jax
import jax.numpy as jnp
from jax import lax
from jax.experimental import pallas as pl
from jax.experimental.pallas import tpu as pltpu

F32 = jnp.float32
BF16 = jnp.bfloat16

EPS = 1e-6
CHUNK = 64
A_HEADS = 8
A_HEAD_DIM = 64
A_WIDTH = A_HEADS * A_HEAD_DIM
A_PAST_CHUNKS = 8
A_BAND = (A_PAST_CHUNKS + 1) * CHUNK
A_MAX_REL = 128
B_HEADS = 4
B_HEAD_DIM = 128
B_WIDTH = B_HEADS * B_HEAD_DIM
CONV_K = 4
N_MOD = 6
LANES = 128
SUBLANES = 8
VMEM_LIMIT = 60 * 1024 * 1024

ROW_TILE = A_PAST_CHUNKS * CHUNK
COL_TILE = 256
FFN_CHUNK = 256
ATTN_PAIRS_PER_PHASE = 4
BIAS_ROW = -(-(CHUNK + A_BAND - 1) // LANES) * LANES
MOD_COL_BLOCKS = 4


def _params(*sem):
    return pltpu.CompilerParams(dimension_semantics=sem, vmem_limit_bytes=VMEM_LIMIT)


def _resident(block_shape, index_map):
    return pl.BlockSpec(block_shape, index_map, pipeline_mode=pl.Buffered(1))


def _mm(a, b):
    return jnp.dot(a.astype(BF16), b.astype(BF16), preferred_element_type=F32)


def _mm_nt(a, b):
    return lax.dot_general(a.astype(BF16), b.astype(BF16), (((1,), (1,)), ((), ())),
                           preferred_element_type=F32)


def _rows(x, g, n=CHUNK):
    return x[g * n:(g + 1) * n]


def _bmm(a, b, groups, rows_a=CHUNK, rows_b=CHUNK):
    return jnp.concatenate([_mm(_rows(a, g, rows_a), _rows(b, g, rows_b)) for g in range(groups)], axis=0)


def _sigmoid(x):
    return 1.0 / (1.0 + jnp.exp(-x))


def _silu(x):
    return x * _sigmoid(x)


def _norm_mod(x, gain, shift, scale):
    ms = jnp.mean(x * x, axis=-1, keepdims=True)
    return x * lax.rsqrt(ms + EPS) * (gain * (1.0 + scale)) + shift


def _lane_bcast(x, lane, width):
    return jnp.broadcast_to(x[:, lane:lane + 1], (x.shape[0], width))


def _interleave(*lists):
    tagged = sorted(((i + 0.5) / len(l), n, i) for n, l in enumerate(lists) for i in range(len(l)))
    return [lists[n][i] for _, n, i in tagged]


def _mod_kernel(c_ref, w_ref, b_ref, o_ref):
    cond = _silu(c_ref[...])
    o_ref[...] = _mm(cond, w_ref[...]) + b_ref[...]


def _modulation(c, w_ada, b_ada):
    depth, d, n = w_ada.shape
    bsz = c.shape[0]
    rows = -(-bsz // SUBLANES) * SUBLANES
    c_pad = jnp.zeros((rows, d), F32).at[:bsz].set(c)
    tn = n // MOD_COL_BLOCKS
    out = pl.pallas_call(
        _mod_kernel,
        grid=(depth, n // tn),
        in_specs=[
            pl.BlockSpec((rows, d), lambda l, j: (0, 0)),
            pl.BlockSpec((None, d, tn), lambda l, j: (l, 0, j)),
            pl.BlockSpec((None, 1, tn), lambda l, j: (l, 0, j)),
        ],
        out_specs=pl.BlockSpec((None, rows, tn), lambda l, j: (l, 0, j)),
        out_shape=jax.ShapeDtypeStruct((depth, rows, n), F32),
        compiler_params=_params("arbitrary", "arbitrary"),
        name="adaln_modulation",
    )(c_pad, w_ada, b_ada.reshape(depth, 1, n))
    return out[:, :bsz].reshape(depth, bsz, N_MOD, d)


def _tail_kernel(a_ref, b_ref, wg_ref, wba_ref):
    n_ba = 2 * B_HEADS
    wa = a_ref.shape[1]
    wba_ref[...] = a_ref[:, 0:LANES]
    wg_ref[...] = jnp.concatenate([a_ref[:, n_ba:wa], b_ref[:, 0:wg_ref.shape[1] - (wa - n_ba)]], axis=1)


def _tail_weights(w_all, n_main, d):
    depth, _, n_all = w_all.shape
    win = n_main // 2
    n_gates = 2 * d
    assert win % LANES == 0 and 2 * win >= 2 * B_HEADS + n_gates and n_all == n_main + 2 * B_HEADS + n_gates
    return pl.pallas_call(
        _tail_kernel,
        grid=(depth,),
        in_specs=[pl.BlockSpec((None, d, win), lambda l: (l, 0, n_main // win)),
                  pl.BlockSpec((None, d, win), lambda l: (l, 0, n_main // win + 1))],
        out_specs=[pl.BlockSpec((None, d, n_gates), lambda l: (l, 0, 0)),
                   pl.BlockSpec((None, d, LANES), lambda l: (l, 0, 0))],
        out_shape=[jax.ShapeDtypeStruct((depth, d, n_gates), BF16), jax.ShapeDtypeStruct((depth, d, LANES), BF16)],
        compiler_params=_params("arbitrary"),
        name="gate_weight_layout",
    )(w_all, w_all)


def _attention_jobs(q_scr, k_scr, v_scr, bias_scr, ya_ref, first):
    tm = q_scr.shape[0]
    pairs = A_WIDTH // LANES
    heads_per_pair = LANES // A_HEAD_DIM
    past = A_PAST_CHUNKS * CHUNK
    lane = lax.broadcasted_iota(jnp.int32, (CHUNK, LANES), 1)
    col = lax.broadcasted_iota(jnp.int32, (1, A_BAND), 1)
    live = {}

    def scores(r0, p0):
        base = r0 + (tm - past)
        parts = []
        for p in range(p0, p0 + ATTN_PAIRS_PER_PHASE):
            cs = slice(p * LANES, (p + 1) * LANES)
            q2 = q_scr[r0:r0 + CHUNK, cs]
            zero = jnp.zeros_like(q2)
            qs = jnp.concatenate(
                [jnp.where((lane >= hh * A_HEAD_DIM) & (lane < (hh + 1) * A_HEAD_DIM), q2, zero)
                 for hh in range(heads_per_pair)], axis=0)
            parts.append(_mm_nt(qs, k_scr[base:base + A_BAND, cs]))
        valid = jnp.logical_or(col + base >= tm, jnp.logical_not(first))
        bias = bias_scr[p0 * LANES:(p0 + ATTN_PAIRS_PER_PHASE) * LANES, :]
        live["s"] = jnp.concatenate(parts, axis=0) + bias + jnp.where(valid, 0.0, -1e30)

    def softmax(r0, p0):
        s = live.pop("s")
        e = jnp.exp(s - jnp.max(s, axis=-1, keepdims=True))
        live["p"] = (e.astype(BF16), 1.0 / jnp.sum(e, axis=-1, keepdims=True))

    def output(r0, p0):
        pb, rinv = live.pop("p")
        base = r0 + (tm - past)
        for n, p in enumerate(range(p0, p0 + ATTN_PAIRS_PER_PHASE)):
            cs = slice(p * LANES, (p + 1) * LANES)
            rs = slice(n * LANES, (n + 1) * LANES)
            o = _mm(pb[rs], v_scr[base:base + A_BAND, cs]) * rinv[rs]
            o2 = jnp.where(lane < A_HEAD_DIM, o[:CHUNK], o[CHUNK:])
            ya_ref[r0:r0 + CHUNK, cs] = o2.astype(ya_ref.dtype)

    return [functools.partial(f, r0, p0) for r0 in range(0, tm, CHUNK)
            for p0 in range(0, pairs, ATTN_PAIRS_PER_PHASE) for f in (scores, softmax, output)]


def _gdn_jobs(x_ref, z_ref, ba_ref, hp_ref, ng_ref, o_ref,
              u_scr, w_scr, qd_scr, qk_scr, kdt_scr, gl_scr, o_scr, state):
    rows = x_ref.shape[0]
    groups = rows // CHUNK
    heads = range(B_HEADS)
    v_ = {}

    def pair_mask(b):
        ri, ci = v_["ri"], v_["ci"]
        return ((ri ^ ci) < 2 * b) & ((ri & b) != 0) & ((ci & b) == 0)

    def setup():
        ri = lax.broadcasted_iota(jnp.int32, (rows, CHUNK), 0) & (CHUNK - 1)
        ci = lax.broadcasted_iota(jnp.int32, (rows, CHUNK), 1)
        r1 = lax.broadcasted_iota(jnp.int32, (CHUNK, CHUNK), 0)
        c1 = lax.broadcasted_iota(jnp.int32, (CHUNK, CHUNK), 1)
        tril = jnp.where(r1 >= c1, 1.0, 0.0).astype(F32)
        v_.update(ri=ri, ci=ci, ts=[], ps=[], rhss=[])
        ba = ba_ref[...]
        sp = ba + hp_ref[1:2, :]
        softplus = jnp.maximum(sp, 0.0) + jnp.log(1.0 + jnp.exp(-jnp.abs(sp)))
        g_all = -jnp.exp(hp_ref[0:1, :]) * softplus
        gc_all = jnp.concatenate(
            [jnp.dot(tril, _rows(g_all, g), preferred_element_type=F32, precision=lax.Precision.HIGHEST)
             for g in range(groups)], axis=0)
        gc_last = jnp.concatenate(
            [jnp.broadcast_to(gc_all[(g + 1) * CHUNK - 1:(g + 1) * CHUNK, :], (CHUNK, LANES))
             for g in range(groups)], axis=0)
        v_.update(beta_all=_sigmoid(ba), g_all=g_all, gc_all=gc_all, e_gc=jnp.exp(gc_all),
                  e_rev=jnp.exp(gc_last - gc_all), e_last=jnp.exp(gc_last))

    def prep(h):
        ri, ci = v_["ri"], v_["ci"]
        q = x_ref[:, h * B_HEAD_DIM:(h + 1) * B_HEAD_DIM]
        k = x_ref[:, B_WIDTH + h * B_HEAD_DIM:B_WIDTH + (h + 1) * B_HEAD_DIM]
        v = x_ref[:, 2 * B_WIDTH + h * B_HEAD_DIM:2 * B_WIDTH + (h + 1) * B_HEAD_DIM]
        q = q * (lax.rsqrt(jnp.sum(q * q, axis=-1, keepdims=True) + EPS) * (B_HEAD_DIM ** -0.5))
        k = k * lax.rsqrt(jnp.sum(k * k, axis=-1, keepdims=True) + EPS)
        beta = _lane_bcast(v_["beta_all"], h, B_HEAD_DIM)
        egc = _lane_bcast(v_["e_gc"], B_HEADS + h, B_HEAD_DIM)
        gmat = _lane_bcast(v_["g_all"], B_HEADS + h, CHUNK)
        gc_col = _lane_bcast(v_["gc_all"], B_HEADS + h, CHUNK)
        gc_row = jnp.sum(jnp.where(ri <= ci, gmat, 0.0).reshape(groups, CHUNK, CHUNK), axis=1, keepdims=True)
        gc_row = jnp.broadcast_to(gc_row, (groups, CHUNK, CHUNK)).reshape(rows, CHUNK)
        decay = jnp.exp(jnp.where(ri >= ci, gc_col - gc_row, -jnp.inf))
        kb, qb = k.astype(BF16), q.astype(BF16)
        both = [_mm_nt(jnp.concatenate([_rows(kb, g), _rows(qb, g)], axis=0), _rows(kb, g)) for g in range(groups)]
        kk = jnp.concatenate([r[:CHUNK] for r in both], axis=0)
        qk = jnp.concatenate([r[CHUNK:] for r in both], axis=0)
        a = jnp.where(ri > ci, beta[:, :CHUNK] * kk * decay, 0.0)
        eye = jnp.where(ri == ci, 1.0, 0.0).astype(F32)
        v_["ts"].append(eye - jnp.where(pair_mask(1), a, 0.0))
        v_["ps"].append(a)
        v_["rhss"].append(jnp.concatenate([v * beta, k * (beta * egc)], axis=-1))
        k_dec = k * _lane_bcast(v_["e_rev"], B_HEADS + h, B_HEAD_DIM)
        qd_scr[h] = (q * egc).astype(BF16)
        qk_scr[h] = (qk * decay).astype(BF16)
        kdt_scr[h] = jnp.concatenate([_rows(k_dec, g).T for g in range(groups)], axis=0).astype(BF16)
        e_last = _lane_bcast(v_["e_last"], B_HEADS + h, B_HEAD_DIM)
        gl_scr[h] = jnp.concatenate([e_last[g * CHUNK:g * CHUNK + SUBLANES] for g in range(groups)], axis=0)

    def level(b):
        below = pair_mask(b)
        ns = [jnp.where(below, a, 0.0) for a in v_["ps"]]
        ws = [_bmm(n, t, groups) for n, t in zip(ns, v_["ts"])]
        v_["ts"] = [t - _bmm(t, w, groups) for t, w in zip(v_["ts"], ws)]

    def solve(h):
        sol = _bmm(v_["ts"][h], v_["rhss"][h], groups)
        u_scr[h] = sol[:, :B_HEAD_DIM]
        w_scr[h] = sol[:, B_HEAD_DIM:].astype(BF16)

    def step(c):
        blk = lambda ref, h, n=CHUNK: ref[h, c * n:(c + 1) * n, :]
        s_old = [state[h] for h in heads]
        r1s = [_mm(jnp.concatenate([blk(w_scr, h), blk(qd_scr, h)], axis=0), s) for h, s in zip(heads, s_old)]
        us = [blk(u_scr, h) - r[:CHUNK] for h, r in zip(heads, r1s)]
        r2s = [_mm(jnp.concatenate([blk(qk_scr, h), blk(kdt_scr, h, B_HEAD_DIM)], axis=0), u)
               for h, u in zip(heads, us)]
        for h, s, ra, rb in zip(heads, s_old, r1s, r2s):
            gl = blk(gl_scr, h, SUBLANES)
            state[h] = s * jnp.concatenate([gl] * (B_HEAD_DIM // SUBLANES), axis=0) + rb[CHUNK:]
            o_scr[c * CHUNK:(c + 1) * CHUNK, h * B_HEAD_DIM:(h + 1) * B_HEAD_DIM] = ra[CHUNK:] + rb[:CHUNK]

    def finish(h):
        cs = slice(h * B_HEAD_DIM, (h + 1) * B_HEAD_DIM)
        o = o_scr[:, cs]
        o = o * lax.rsqrt(jnp.mean(o * o, axis=-1, keepdims=True) + EPS) * ng_ref[...] * _silu(z_ref[:, cs])
        o_ref[:, cs] = o.astype(o_ref.dtype)

    levels = []
    b = 2
    while b < CHUNK:
        levels.append(functools.partial(level, b))
        b *= 2
    return ([setup] + [functools.partial(prep, h) for h in heads] + levels
            + [functools.partial(solve, h) for h in heads] + [functools.partial(step, c) for c in range(groups)]
            + [functools.partial(finish, h) for h in heads])


def _mixing_kernel(x_ref, mod_ref, g_ref, wm_ref, wg_ref, wba_ref, wc_ref, brow_ref, hp_ref, ng_ref,
                   ya_ref, yb_ref, og_ref,
                   xext, q_scr, k_scr, v_scr, bias_scr, qkv_scr, z_scr, ba_scr,
                   u_scr, w_scr, qd_scr, qk_scr, kdt_scr, gl_scr, o_scr, state,
                   *, tiles_per_batch):
    tm = x_ref.shape[0]
    a3, b3 = 3 * A_WIDTH, 3 * B_WIDTH
    first = pl.program_id(0) % tiles_per_batch == 0

    @pl.when(pl.program_id(0) == 0)
    def _():
        for hd in range(A_HEADS):
            tiled = jnp.broadcast_to(brow_ref[hd:hd + 1, :], (CHUNK, BIAS_ROW))
            rolled = pltpu.roll(tiled, BIAS_ROW - (CHUNK - 1), 1, stride=1, stride_axis=0)
            bias_scr[hd * CHUNK:(hd + 1) * CHUNK, :] = rolled[:, :A_BAND]
        k_scr[...] = jnp.zeros_like(k_scr)
        v_scr[...] = jnp.zeros_like(v_scr)

    @pl.when(first)
    def _():
        xext[0:SUBLANES, :] = jnp.zeros((SUBLANES, b3), F32)
        state[...] = jnp.zeros_like(state)

    k_scr[0:tm, :] = k_scr[tm:2 * tm, :]
    v_scr[0:tm, :] = v_scr[tm:2 * tm, :]

    h = _norm_mod(x_ref[...], g_ref[...], mod_ref[0:1, :], mod_ref[1:2, :]).astype(BF16)

    def proj(w_ref, c0, cw):
        return jnp.dot(h, w_ref[:, c0:c0 + cw], preferred_element_type=F32)

    scale = A_HEAD_DIM ** -0.5
    for c0 in range(0, A_WIDTH, COL_TILE):
        q_scr[:, c0:c0 + COL_TILE] = proj(wm_ref, c0, COL_TILE).astype(BF16) * scale
        k_scr[tm:2 * tm, c0:c0 + COL_TILE] = proj(wm_ref, A_WIDTH + c0, COL_TILE).astype(BF16)
        v_scr[tm:2 * tm, c0:c0 + COL_TILE] = proj(wm_ref, 2 * A_WIDTH + c0, COL_TILE).astype(BF16)

    def conv_tile(c0):
        full = xext[:, c0:c0 + LANES]
        acc = wc_ref[CONV_K - 1:CONV_K, c0:c0 + LANES] * full[SUBLANES:]
        for s in range(1, CONV_K):
            shifted = pltpu.roll(full, s, 0)[SUBLANES:]
            acc = acc + wc_ref[CONV_K - 1 - s:CONV_K - s, c0:c0 + LANES] * shifted
        qkv_scr[:, c0:c0 + LANES] = _silu(acc)

    def raw_job(r0):
        xext[SUBLANES:SUBLANES + tm, r0:r0 + COL_TILE] = proj(wm_ref, a3 + r0, COL_TILE)
        for c0 in range(r0, r0 + COL_TILE, LANES):
            conv_tile(c0)

    def plain_job(o_ref, w_ref, wc0, oc0):
        cw = min(COL_TILE, o_ref.shape[1])
        o_ref[:, oc0:oc0 + cw] = proj(w_ref, wc0, cw).astype(o_ref.dtype)

    attn = _attention_jobs(q_scr, k_scr, v_scr, bias_scr, ya_ref, first)
    gdn = _gdn_jobs(qkv_scr, z_scr, ba_scr, hp_ref, ng_ref, yb_ref,
                    u_scr, w_scr, qd_scr, qk_scr, kdt_scr, gl_scr, o_scr, state)
    feed = [functools.partial(raw_job, r0) for r0 in range(0, b3, COL_TILE)]
    feed += [functools.partial(plain_job, z_scr, wm_ref, a3 + b3 + c0, c0) for c0 in range(0, B_WIDTH, COL_TILE)]
    feed += [functools.partial(plain_job, ba_scr, wba_ref, 0, 0)]
    gates = [functools.partial(plain_job, og_ref, wg_ref, c0, c0) for c0 in range(0, og_ref.shape[1], COL_TILE)]
    n_early = len(attn) * len(feed) // (len(feed) + len(gates))
    for job in _interleave(feed, attn[:n_early]) + _interleave(gates, gdn, attn[n_early:]):
        job()
    xext[0:SUBLANES, :] = xext[tm:tm + SUBLANES, :]


def _token_mixing(x2, mod_l, gain, w_all, w_gates, w_ba, w_conv, bias_rows, head_params, norm_g, layer,
                  rows_per_batch):
    m, d = x2.shape
    tm = ROW_TILE
    tpb = rows_per_batch // tm
    groups = tm // CHUNK
    widths = (A_WIDTH, B_WIDTH, 2 * d)
    dtypes = (BF16, BF16, BF16)
    row_spec = lambda w: pl.BlockSpec((tm, w), lambda i: (i, 0))
    head_scr = lambda r, w, t: pltpu.VMEM((B_HEADS, r, w), t)
    return pl.pallas_call(
        functools.partial(_mixing_kernel, tiles_per_batch=tpb),
        grid=(m // tm,),
        in_specs=[
            row_spec(d),
            pl.BlockSpec((None, N_MOD, d), lambda i: (i // tpb, 0, 0)),
            _resident((None, 1, d), lambda i: (layer, 0, 0)),
            _resident((None, d, 3 * A_WIDTH + 4 * B_WIDTH), lambda i: (layer, 0, 0)),
            _resident((None, d, 2 * d), lambda i: (layer, 0, 0)),
            _resident((None, d, LANES), lambda i: (layer, 0, 0)),
            _resident((None, CONV_K, 3 * B_WIDTH), lambda i: (layer, 0, 0)),
            _resident((None, A_HEADS, BIAS_ROW), lambda i: (layer, 0, 0)),
            _resident((None, 2, LANES), lambda i: (layer, 0, 0)),
            _resident((None, 1, B_HEAD_DIM), lambda i: (layer, 0, 0)),
        ],
        out_specs=[row_spec(w) for w in widths],
        out_shape=[jax.ShapeDtypeStruct((m, w), t) for w, t in zip(widths, dtypes)],
        scratch_shapes=[pltpu.VMEM((tm + SUBLANES, 3 * B_WIDTH), F32),
                        pltpu.VMEM((tm, A_WIDTH), BF16),
                        pltpu.VMEM((2 * tm, A_WIDTH), BF16), pltpu.VMEM((2 * tm, A_WIDTH), BF16),
                        pltpu.VMEM((A_HEADS * CHUNK, A_BAND), F32),
                        pltpu.VMEM((tm, 3 * B_WIDTH), F32), pltpu.VMEM((tm, B_WIDTH), F32),
                        pltpu.VMEM((tm, LANES), F32),
                        head_scr(tm, B_HEAD_DIM, F32), head_scr(tm, B_HEAD_DIM, BF16),
                        head_scr(tm, B_HEAD_DIM, BF16), head_scr(tm, CHUNK, BF16),
                        head_scr(groups * B_HEAD_DIM, CHUNK, BF16), head_scr(groups * SUBLANES, B_HEAD_DIM, F32),
                        pltpu.VMEM((tm, B_WIDTH), F32),
                        pltpu.VMEM((B_HEADS, B_HEAD_DIM, B_HEAD_DIM), F32)],
        compiler_params=_params("arbitrary"),
        name="token_mixing",
    )(x2, mod_l, gain, w_all, w_gates, w_ba, w_conv, bias_rows, head_params, norm_g)


def _mix_ffn_kernel(x_ref, ya_ref, yb_ref, gt_ref, mod_ref, g_ref, fg_ref, wa_ref, wb_ref, wm_ref, wi_ref, wo_ref,
                    o_ref, x1_ref, acc_ref, *, final):
    d = x_ref.shape[-1]
    gate_a = _sigmoid(gt_ref[:, 0:d].astype(F32))
    gate_b = _sigmoid(gt_ref[:, d:2 * d].astype(F32))
    merged = (gate_a * jnp.dot(ya_ref[...], wa_ref[...], preferred_element_type=F32)
              + gate_b * jnp.dot(yb_ref[...], wb_ref[...], preferred_element_type=F32))
    x1_ref[...] = x_ref[...] + mod_ref[2:3, :] * _mm(merged, wm_ref[...])
    h = _norm_mod(x1_ref[...], g_ref[...], mod_ref[3:4, :], mod_ref[4:5, :]).astype(BF16)
    hidden = wo_ref.shape[0]
    for j in range(hidden // FFN_CHUNK):
        js = slice(j * FFN_CHUNK, (j + 1) * FFN_CHUNK)
        gate = jnp.dot(h, wi_ref[:, js], preferred_element_type=F32)
        up = jnp.dot(h, wi_ref[:, hidden + j * FFN_CHUNK:hidden + (j + 1) * FFN_CHUNK],
                     preferred_element_type=F32)
        act = (_silu(gate) * up).astype(BF16)
        part = jnp.dot(act, wo_ref[js, :], preferred_element_type=F32)
        if j == 0:
            acc_ref[...] = part
        else:
            acc_ref[...] += part
    out = x1_ref[...] + mod_ref[5:6, :] * acc_ref[...]
    if final:
        out = out * lax.rsqrt(jnp.mean(out * out, axis=-1, keepdims=True) + EPS) * fg_ref[...]
    o_ref[...] = out


def _mix_ffn(x2, ya, yb, gates, mod_l, gain, final_g, wa, wb, wm, wi, wo, layer, rows_per_batch, final):
    m, d = x2.shape
    tm = ROW_TILE
    tpb = rows_per_batch // tm
    hidden = wo.shape[1]
    row_spec = lambda w: pl.BlockSpec((tm, w), lambda i: (i, 0))
    return pl.pallas_call(
        functools.partial(_mix_ffn_kernel, final=final),
        grid=(m // tm,),
        in_specs=[row_spec(d), row_spec(A_WIDTH), row_spec(B_WIDTH), row_spec(2 * d),
                  pl.BlockSpec((None, N_MOD, d), lambda i: (i // tpb, 0, 0)),
                  _resident((None, 1, d), lambda i: (layer, 0, 0)),
                  _resident((1, d), lambda i: (0, 0)),
                  _resident((None, A_WIDTH, d), lambda i: (layer, 0, 0)),
                  _resident((None, B_WIDTH, d), lambda i: (layer, 0, 0)),
                  _resident((None, d, d), lambda i: (layer, 0, 0)),
                  _resident((None, d, 2 * hidden), lambda i: (layer, 0, 0)),
                  _resident((None, hidden, d), lambda i: (layer, 0, 0))],
        out_specs=row_spec(d),
        out_shape=jax.ShapeDtypeStruct((m, d), F32),
        scratch_shapes=[pltpu.VMEM((tm, d), F32), pltpu.VMEM((tm, d), F32)],
        compiler_params=_params("arbitrary"),
        name="merge_swiglu",
    )(x2, ya, yb, gates, mod_l, gain, final_g, wa, wb, wm, wi, wo)


def _bias_rows(rel_table):
    t = rel_table.astype(F32)
    lo = A_MAX_REL - (CHUNK - 1)
    ramp = t[..., lo:][..., ::-1]
    n_far = CHUNK + A_BAND - 1 - ramp.shape[-1]
    far = jnp.broadcast_to(t[..., -1:], t.shape[:-1] + (n_far,))
    pad = jnp.zeros(t.shape[:-1] + (BIAS_ROW - (CHUNK + A_BAND - 1),), F32)
    return jnp.concatenate([far, ramp, pad], axis=-1)


def kernel(x, c, w_ada, b_ada, norm1_g, norm2_g, w_in, rel_table, w_conv, a_log, dt_bias, gdn_norm_g,
           w_branch_a, w_branch_b, w_out, w_ffn_in, w_ffn_out, final_g):
    bsz, seq, d = x.shape
    depth = w_ada.shape[0]
    assert seq % ROW_TILE == 0 and d % LANES == 0

    mod = _modulation(c, w_ada, b_ada)
    n_main = 3 * A_WIDTH + 4 * B_WIDTH
    w_all = w_in.astype(BF16)
    w_gates, w_ba = _tail_weights(w_all, n_main, d)
    wi, wo_ffn = w_ffn_in.astype(BF16), w_ffn_out.astype(BF16)
    wa, wb, wo = w_branch_a.astype(BF16), w_branch_b.astype(BF16), w_out.astype(BF16)
    bias_rows = _bias_rows(rel_table)
    lane_pad = jnp.zeros((depth, LANES - 2 * B_HEADS), F32)
    zeros_h = jnp.zeros((depth, B_HEADS), F32)
    head_params = jnp.stack([jnp.concatenate([zeros_h, a_log, lane_pad], axis=-1),
                             jnp.concatenate([zeros_h, dt_bias, lane_pad], axis=-1)], axis=1)
    g1 = norm1_g.reshape(depth, 1, d)
    g2 = norm2_g.reshape(depth, 1, d)
    ng = gdn_norm_g.reshape(depth, 1, B_HEAD_DIM)

    x2 = x.reshape(bsz * seq, d)
    for l in range(depth):
        ya, yb, gates = _token_mixing(x2, mod[l], g1, w_all, w_gates, w_ba, w_conv, bias_rows, head_params, ng, l,
                                      seq)
        x2 = _mix_ffn(x2, ya, yb, gates, mod[l], g2, final_g.reshape(1, d),
                      wa, wb, wo, wi, wo_ffn, l, seq, final=(l == depth - 1))
    return x2.reshape(bsz, seq, d)
```

```python
import functools

import jax
import jax.numpy as jnp
from jax import lax
from jax.experimental import pallas as pl
from jax.experimental.pallas import tpu as pltpu

F32 = jnp.float32
BF16 = jnp.bfloat16

EPS = 1e-6
CHUNK = 64
A_HEADS = 8
A_HEAD_DIM = 64
A_WIDTH = A_HEADS * A_HEAD_DIM
A_PAST_CHUNKS = 8
A_BAND = (A_PAST_CHUNKS + 1) * CHUNK
A_MAX_REL = 128
B_HEADS = 4
B_HEAD_DIM = 128
B_WIDTH = B_HEADS * B_HEAD_DIM
CONV_K = 4
N_MOD = 6
LANES = 128
SUBLANES = 8
VMEM_LIMIT = 60 * 1024 * 1024

ROW_TILE = A_PAST_CHUNKS * CHUNK
COL_TILE = 256
FFN_CHUNK = 256
ATTN_PAIRS_PER_PHASE = 4
BIAS_ROW = -(-(CHUNK + A_BAND - 1) // LANES) * LANES
MOD_COL_BLOCKS = 4


def _params(*sem):
    return pltpu.CompilerParams(dimension_semantics=sem, vmem_limit_bytes=VMEM_LIMIT)


def _resident(block_shape, index_map):
    return pl.BlockSpec(block_shape, index_map, pipeline_mode=pl.Buffered(1))


def _mm(a, b):
    return jnp.dot(a.astype(BF16), b.astype(BF16), preferred_element_type=F32)


def _mm_nt(a, b):
    return lax.dot_general(a.astype(BF16), b.astype(BF16), (((1,), (1,)), ((), ())),
                           preferred_element_type=F32)


def _rows(x, g, n=CHUNK):
    return x[g * n:(g + 1) * n]


def _bmm(a, b, groups, rows_a=CHUNK, rows_b=CHUNK):
    return jnp.concatenate([_mm(_rows(a, g, rows_a), _rows(b, g, rows_b)) for g in range(groups)], axis=0)


def _sigmoid(x):
    return 1.0 / (1.0 + jnp.exp(-x))


def _silu(x):
    return x * _sigmoid(x)


def _norm_mod(x, gain, shift, scale):
    ms = jnp.mean(x * x, axis=-1, keepdims=True)
    return x * lax.rsqrt(ms + EPS) * (gain * (1.0 + scale)) + shift


def _lane_bcast(x, lane, width):
    return jnp.broadcast_to(x[:, lane:lane + 1], (x.shape[0], width))


def _interleave(*lists):
    tagged = sorted(((i + 0.5) / len(l), n, i) for n, l in enumerate(lists) for i in range(len(l)))
    return [lists[n][i] for _, n, i in tagged]


def _mod_kernel(c_ref, w_ref, b_ref, o_ref):
    cond = _silu(c_ref[...])
    o_ref[...] = _mm(cond, w_ref[...]) + b_ref[...]


def _modulation(c, w_ada, b_ada):
    depth, d, n = w_ada.shape
    bsz = c.shape[0]
    rows = -(-bsz // SUBLANES) * SUBLANES
    c_pad = jnp.zeros((rows, d), F32).at[:bsz].set(c)
    tn = n // MOD_COL_BLOCKS
    out = pl.pallas_call(
        _mod_kernel,
        grid=(depth, n // tn),
        in_specs=[
            pl.BlockSpec((rows, d), lambda l, j: (0, 0)),
            pl.BlockSpec((None, d, tn), lambda l, j: (l, 0, j)),
            pl.BlockSpec((None, 1, tn), lambda l, j: (l, 0, j)),
        ],
        out_specs=pl.BlockSpec((None, rows, tn), lambda l, j: (l, 0, j)),
        out_shape=jax.ShapeDtypeStruct((depth, rows, n), F32),
        compiler_params=_params("arbitrary", "arbitrary"),
        name="adaln_modulation",
    )(c_pad, w_ada, b_ada.reshape(depth, 1, n))
    return out[:, :bsz].reshape(depth, bsz, N_MOD, d)


def _tail_kernel(a_ref, b_ref, wg_ref, wba_ref):
    n_ba = 2 * B_HEADS
    wa = a_ref.shape[1]
    wba_ref[...] = a_ref[:, 0:LANES]
    wg_ref[...] = jnp.concatenate([a_ref[:, n_ba:wa], b_ref[:, 0:wg_ref.shape[1] - (wa - n_ba)]], axis=1)


def _tail_weights(w_all, n_main, d):
    depth, _, n_all = w_all.shape
    win = n_main // 2
    n_gates = 2 * d
    assert win % LANES == 0 and 2 * win >= 2 * B_HEADS + n_gates and n_all >= n_main + 2 * B_HEADS + n_gates
    return pl.pallas_call(
        _tail_kernel,
        grid=(depth,),
        in_specs=[pl.BlockSpec((None, d, win), lambda l: (l, 0, n_main // win)),
                  pl.BlockSpec((None, d, win), lambda l: (l, 0, n_main // win + 1))],
        out_specs=[pl.BlockSpec((None, d, n_gates), lambda l: (l, 0, 0)),
                   pl.BlockSpec((None, d, LANES), lambda l: (l, 0, 0))],
        out_shape=[jax.ShapeDtypeStruct((depth, d, n_gates), BF16), jax.ShapeDtypeStruct((depth, d, LANES), BF16)],
        compiler_params=_params("arbitrary"),
        name="gate_weight_layout",
    )(w_all, w_all)


def _attention_jobs(q_scr, k_scr, v_scr, bias_scr, ya_ref, first):
    tm = q_scr.shape[0]
    pairs = A_WIDTH // LANES
    heads_per_pair = LANES // A_HEAD_DIM
    past = A_PAST_CHUNKS * CHUNK
    lane = lax.broadcasted_iota(jnp.int32, (CHUNK, LANES), 1)
    col = lax.broadcasted_iota(jnp.int32, (1, A_BAND), 1)
    live = {}

    def scores(r0, p0):
        base = r0 + (tm - past)
        parts = []
        for p in range(p0, p0 + ATTN_PAIRS_PER_PHASE):
            cs = slice(p * LANES, (p + 1) * LANES)
            q2 = q_scr[r0:r0 + CHUNK, cs]
            zero = jnp.zeros_like(q2)
            qs = jnp.concatenate(
                [jnp.where((lane >= hh * A_HEAD_DIM) & (lane < (hh + 1) * A_HEAD_DIM), q2, zero)
                 for hh in range(heads_per_pair)], axis=0)
            parts.append(_mm_nt(qs, k_scr[base:base + A_BAND, cs]))
        valid = jnp.logical_or(col + base >= tm, jnp.logical_not(first))
        bias = bias_scr[p0 * LANES:(p0 + ATTN_PAIRS_PER_PHASE) * LANES, :]
        live["s"] = jnp.concatenate(parts, axis=0) + bias + jnp.where(valid, 0.0, -1e30)

    def softmax(r0, p0):
        s = live.pop("s")
        e = jnp.exp(s - jnp.max(s, axis=-1, keepdims=True))
        live["p"] = (e.astype(BF16), 1.0 / jnp.sum(e, axis=-1, keepdims=True))

    def output(r0, p0):
        pb, rinv = live.pop("p")
        base = r0 + (tm - past)
        for n, p in enumerate(range(p0, p0 + ATTN_PAIRS_PER_PHASE)):
            cs = slice(p * LANES, (p + 1) * LANES)
            rs = slice(n * LANES, (n + 1) * LANES)
            o = _mm(pb[rs], v_scr[base:base + A_BAND, cs]) * rinv[rs]
            o2 = jnp.where(lane < A_HEAD_DIM, o[:CHUNK], o[CHUNK:])
            ya_ref[r0:r0 + CHUNK, cs] = o2.astype(ya_ref.dtype)

    return [functools.partial(f, r0, p0) for r0 in range(0, tm, CHUNK)
            for p0 in range(0, pairs, ATTN_PAIRS_PER_PHASE) for f in (scores, softmax, output)]


def _gdn_jobs(x_ref, z_ref, ba_ref, hp_ref, ng_ref, o_ref,
              u_scr, w_scr, qd_scr, qk_scr, kdt_scr, gl_scr, o_scr, state):
    rows = x_ref.shape[0]
    groups = rows // CHUNK
    heads = range(B_HEADS)
    v_ = {}

    def pair_mask(b):
        ri, ci = v_["ri"], v_["ci"]
        return ((ri ^ ci) < 2 * b) & ((ri & b) != 0) & ((ci & b) == 0)

    def setup():
        ri = lax.broadcasted_iota(jnp.int32, (rows, CHUNK), 0) & (CHUNK - 1)
        ci = lax.broadcasted_iota(jnp.int32, (rows, CHUNK), 1)
        r1 = lax.broadcasted_iota(jnp.int32, (CHUNK, CHUNK), 0)
        c1 = lax.broadcasted_iota(jnp.int32, (CHUNK, CHUNK), 1)
        tril = jnp.where(r1 >= c1, 1.0, 0.0).astype(F32)
        v_.update(ri=ri, ci=ci, ts=[], ps=[], rhss=[])
        ba = ba_ref[...]
        sp = ba + hp_ref[1:2, :]
        softplus = jnp.maximum(sp, 0.0) + jnp.log(1.0 + jnp.exp(-jnp.abs(sp)))
        g_all = -jnp.exp(hp_ref[0:1, :]) * softplus
        gc_all = jnp.concatenate(
            [jnp.dot(tril, _rows(g_all, g), preferred_element_type=F32, precision=lax.Precision.HIGHEST)
             for g in range(groups)], axis=0)
        gc_last = jnp.concatenate(
            [jnp.broadcast_to(gc_all[(g + 1) * CHUNK - 1:(g + 1) * CHUNK, :], (CHUNK, LANES))
             for g in range(groups)], axis=0)
        v_.update(beta_all=_sigmoid(ba), g_all=g_all, gc_all=gc_all, e_gc=jnp.exp(gc_all),
                  e_rev=jnp.exp(gc_last - gc_all), e_last=jnp.exp(gc_last))

    def prep(h):
        ri, ci = v_["ri"], v_["ci"]
        q = x_ref[:, h * B_HEAD_DIM:(h + 1) * B_HEAD_DIM]
        k = x_ref[:, B_WIDTH + h * B_HEAD_DIM:B_WIDTH + (h + 1) * B_HEAD_DIM]
        v = x_ref[:, 2 * B_WIDTH + h * B_HEAD_DIM:2 * B_WIDTH + (h + 1) * B_HEAD_DIM]
        q = q * (lax.rsqrt(jnp.sum(q * q, axis=-1, keepdims=True) + EPS) * (B_HEAD_DIM ** -0.5))
        k = k * lax.rsqrt(jnp.sum(k * k, axis=-1, keepdims=True) + EPS)
        beta = _lane_bcast(v_["beta_all"], h, B_HEAD_DIM)
        egc = _lane_bcast(v_["e_gc"], B_HEADS + h, B_HEAD_DIM)
        gmat = _lane_bcast(v_["g_all"], B_HEADS + h, CHUNK)
        gc_col = _lane_bcast(v_["gc_all"], B_HEADS + h, CHUNK)
        gc_row = jnp.sum(jnp.where(ri <= ci, gmat, 0.0).reshape(groups, CHUNK, CHUNK), axis=1, keepdims=True)
        gc_row = jnp.broadcast_to(gc_row, (groups, CHUNK, CHUNK)).reshape(rows, CHUNK)
        decay = jnp.exp(jnp.where(ri >= ci, gc_col - gc_row, -jnp.inf))
        kb, qb = k.astype(BF16), q.astype(BF16)
        both = [_mm_nt(jnp.concatenate([_rows(kb, g), _rows(qb, g)], axis=0), _rows(kb, g)) for g in range(groups)]
        kk = jnp.concatenate([r[:CHUNK] for r in both], axis=0)
        qk = jnp.concatenate([r[CHUNK:] for r in both], axis=0)
        a = jnp.where(ri > ci, beta[:, :CHUNK] * kk * decay, 0.0)
        eye = jnp.where(ri == ci, 1.0, 0.0).astype(F32)
        v_["ts"].append(eye - jnp.where(pair_mask(1), a, 0.0))
        v_["ps"].append(a)
        v_["rhss"].append(jnp.concatenate([v * beta, k * (beta * egc)], axis=-1))
        k_dec = k * _lane_bcast(v_["e_rev"], B_HEADS + h, B_HEAD_DIM)
        qd_scr[h] = (q * egc).astype(BF16)
        qk_scr[h] = (qk * decay).astype(BF16)
        kdt_scr[h] = jnp.concatenate([_rows(k_dec, g).T for g in range(groups)], axis=0).astype(BF16)
        e_last = _lane_bcast(v_["e_last"], B_HEADS + h, B_HEAD_DIM)
        gl_scr[h] = jnp.concatenate([e_last[g * CHUNK:g * CHUNK + SUBLANES] for g in range(groups)], axis=0)

    def level(b):
        below = pair_mask(b)
        ns = [jnp.where(below, a, 0.0) for a in v_["ps"]]
        ws = [_bmm(n, t, groups) for n, t in zip(ns, v_["ts"])]
        v_["ts"] = [t - _bmm(t, w, groups) for t, w in zip(v_["ts"], ws)]

    def solve(h):
        sol = _bmm(v_["ts"][h], v_["rhss"][h], groups)
        u_scr[h] = sol[:, :B_HEAD_DIM]
        w_scr[h] = sol[:, B_HEAD_DIM:].astype(BF16)

    def step(c):
        blk = lambda ref, h, n=CHUNK: ref[h, c * n:(c + 1) * n, :]
        s_old = [state[h] for h in heads]
        r1s = [_mm(jnp.concatenate([blk(w_scr, h), blk(qd_scr, h)], axis=0), s) for h, s in zip(heads, s_old)]
        us = [blk(u_scr, h) - r[:CHUNK] for h, r in zip(heads, r1s)]
        r2s = [_mm(jnp.concatenate([blk(qk_scr, h), blk(kdt_scr, h, B_HEAD_DIM)], axis=0), u)
               for h, u in zip(heads, us)]
        for h, s, ra, rb in zip(heads, s_old, r1s, r2s):
            gl = blk(gl_scr, h, SUBLANES)
            state[h] = s * jnp.concatenate([gl] * (B_HEAD_DIM // SUBLANES), axis=0) + rb[CHUNK:]
            o_scr[c * CHUNK:(c + 1) * CHUNK, h * B_HEAD_DIM:(h + 1) * B_HEAD_DIM] = ra[CHUNK:] + rb[:CHUNK]

    def finish(h):
        cs = slice(h * B_HEAD_DIM, (h + 1) * B_HEAD_DIM)
        o = o_scr[:, cs]
        o = o * lax.rsqrt(jnp.mean(o * o, axis=-1, keepdims=True) + EPS) * ng_ref[...] * _silu(z_ref[:, cs])
        o_ref[:, cs] = o.astype(o_ref.dtype)

    levels = []
    b = 2
    while b < CHUNK:
        levels.append(functools.partial(level, b))
        b *= 2
    return ([setup] + [functools.partial(prep, h) for h in heads] + levels
            + [functools.partial(solve, h) for h in heads] + [functools.partial(step, c) for c in range(groups)]
            + [functools.partial(finish, h) for h in heads])


def _mixing_kernel(x_ref, mod_ref, g_ref, wm_ref, wg_ref, wba_ref, wc_ref, brow_ref, hp_ref, ng_ref,
                   ya_ref, yb_ref, og_ref,
                   xext, q_scr, k_scr, v_scr, bias_scr, qkv_scr, z_scr, ba_scr,
                   u_scr, w_scr, qd_scr, qk_scr, kdt_scr, gl_scr, o_scr, state,
                   *, tiles_per_batch):
    tm = x_ref.shape[0]
    a3, b3 = 3 * A_WIDTH, 3 * B_WIDTH
    first = pl.program_id(0) % tiles_per_batch == 0

    @pl.when(pl.program_id(0) == 0)
    def _():
        for hd in range(A_HEADS):
            tiled = jnp.broadcast_to(brow_ref[hd:hd + 1, :], (CHUNK, BIAS_ROW))
            rolled = pltpu.roll(tiled, BIAS_ROW - (CHUNK - 1), 1, stride=1, stride_axis=0)
            bias_scr[hd * CHUNK:(hd + 1) * CHUNK, :] = rolled[:, :A_BAND]
        k_scr[...] = jnp.zeros_like(k_scr)
        v_scr[...] = jnp.zeros_like(v_scr)

    @pl.when(first)
    def _():
        xext[0:SUBLANES, :] = jnp.zeros((SUBLANES, b3), F32)
        state[...] = jnp.zeros_like(state)

    k_scr[0:tm, :] = k_scr[tm:2 * tm, :]
    v_scr[0:tm, :] = v_scr[tm:2 * tm, :]

    h = _norm_mod(x_ref[...], g_ref[...], mod_ref[0:1, :], mod_ref[1:2, :]).astype(BF16)

    def proj(w_ref, c0, cw):
        return jnp.dot(h, w_ref[:, c0:c0 + cw], preferred_element_type=F32)

    scale = A_HEAD_DIM ** -0.5
    for c0 in range(0, A_WIDTH, COL_TILE):
        q_scr[:, c0:c0 + COL_TILE] = proj(wm_ref, c0, COL_TILE).astype(BF16) * scale
        k_scr[tm:2 * tm, c0:c0 + COL_TILE] = proj(wm_ref, A_WIDTH + c0, COL_TILE).astype(BF16)
        v_scr[tm:2 * tm, c0:c0 + COL_TILE] = proj(wm_ref, 2 * A_WIDTH + c0, COL_TILE).astype(BF16)

    def conv_tile(c0):
        full = xext[:, c0:c0 + LANES]
        acc = wc_ref[CONV_K - 1:CONV_K, c0:c0 + LANES] * full[SUBLANES:]
        for s in range(1, CONV_K):
            shifted = pltpu.roll(full, s, 0)[SUBLANES:]
            acc = acc + wc_ref[CONV_K - 1 - s:CONV_K - s, c0:c0 + LANES] * shifted
        qkv_scr[:, c0:c0 + LANES] = _silu(acc)

    def raw_job(r0):
        xext[SUBLANES:SUBLANES + tm, r0:r0 + COL_TILE] = proj(wm_ref, a3 + r0, COL_TILE)
        for c0 in range(r0, r0 + COL_TILE, LANES):
            conv_tile(c0)

    def plain_job(o_ref, w_ref, wc0, oc0):
        cw = min(COL_TILE, o_ref.shape[1])
        o_ref[:, oc0:oc0 + cw] = proj(w_ref, wc0, cw).astype(o_ref.dtype)

    attn = _attention_jobs(q_scr, k_scr, v_scr, bias_scr, ya_ref, first)
    gdn = _gdn_jobs(qkv_scr, z_scr, ba_scr, hp_ref, ng_ref, yb_ref,
                    u_scr, w_scr, qd_scr, qk_scr, kdt_scr, gl_scr, o_scr, state)
    feed = [functools.partial(raw_job, r0) for r0 in range(0, b3, COL_TILE)]
    feed += [functools.partial(plain_job, z_scr, wm_ref, a3 + b3 + c0, c0) for c0 in range(0, B_WIDTH, COL_TILE)]
    feed += [functools.partial(plain_job, ba_scr, wba_ref, 0, 0)]
    gates = [functools.partial(plain_job, og_ref, wg_ref, c0, c0) for c0 in range(0, og_ref.shape[1], COL_TILE)]
    n_early = len(attn) * len(feed) // (len(feed) + len(gates))
    for job in _interleave(feed, attn[:n_early]) + _interleave(gates, gdn, attn[n_early:]):
        job()
    xext[0:SUBLANES, :] = xext[tm:tm + SUBLANES, :]


def _token_mixing(x2, mod_l, gain, w_all, w_gates, w_ba, w_conv, bias_rows, head_params, norm_g, layer,
                  rows_per_batch):
    m, d = x2.shape
    tm = ROW_TILE
    tpb = rows_per_batch // tm
    groups = tm // CHUNK
    widths = (A_WIDTH, B_WIDTH, 2 * d)
    dtypes = (BF16, BF16, BF16)
    row_spec = lambda w: pl.BlockSpec((tm, w), lambda i: (i, 0))
    head_scr = lambda r, w, t: pltpu.VMEM((B_HEADS, r, w), t)
    return pl.pallas_call(
        functools.partial(_mixing_kernel, tiles_per_batch=tpb),
        grid=(m // tm,),
        in_specs=[
            row_spec(d),
            pl.BlockSpec((None, N_MOD, d), lambda i: (i // tpb, 0, 0)),
            _resident((None, 1, d), lambda i: (layer, 0, 0)),
            _resident((None, d, 3 * A_WIDTH + 4 * B_WIDTH), lambda i: (layer, 0, 0)),
            _resident((None, d, 2 * d), lambda i: (layer, 0, 0)),
            _resident((None, d, LANES), lambda i: (layer, 0, 0)),
            _resident((None, CONV_K, 3 * B_WIDTH), lambda i: (layer, 0, 0)),
            _resident((None, A_HEADS, BIAS_ROW), lambda i: (layer, 0, 0)),
            _resident((None, 2, LANES), lambda i: (layer, 0, 0)),
            _resident((None, 1, B_HEAD_DIM), lambda i: (layer, 0, 0)),
        ],
        out_specs=[row_spec(w) for w in widths],
        out_shape=[jax.ShapeDtypeStruct((m, w), t) for w, t in zip(widths, dtypes)],
        scratch_shapes=[pltpu.VMEM((tm + SUBLANES, 3 * B_WIDTH), F32),
                        pltpu.VMEM((tm, A_WIDTH), BF16),
                        pltpu.VMEM((2 * tm, A_WIDTH), BF16), pltpu.VMEM((2 * tm, A_WIDTH), BF16),
                        pltpu.VMEM((A_HEADS * CHUNK, A_BAND), F32),
                        pltpu.VMEM((tm, 3 * B_WIDTH), F32), pltpu.VMEM((tm, B_WIDTH), F32),
                        pltpu.VMEM((tm, LANES), F32),
                        head_scr(tm, B_HEAD_DIM, F32), head_scr(tm, B_HEAD_DIM, BF16),
                        head_scr(tm, B_HEAD_DIM, BF16), head_scr(tm, CHUNK, BF16),
                        head_scr(groups * B_HEAD_DIM, CHUNK, BF16), head_scr(groups * SUBLANES, B_HEAD_DIM, F32),
                        pltpu.VMEM((tm, B_WIDTH), F32),
                        pltpu.VMEM((B_HEADS, B_HEAD_DIM, B_HEAD_DIM), F32)],
        compiler_params=_params("arbitrary"),
        name="token_mixing",
    )(x2, mod_l, gain, w_all, w_gates, w_ba, w_conv, bias_rows, head_params, norm_g)


def _mix_ffn_kernel(x_ref, ya_ref, yb_ref, gt_ref, mod_ref, g_ref, fg_ref, wa_ref, wb_ref, wm_ref, wi_ref, wo_ref,
                    o_ref, x1_ref, acc_ref, *, final):
    d = x_ref.shape[-1]
    gate_a = _sigmoid(gt_ref[:, 0:d].astype(F32))
    gate_b = _sigmoid(gt_ref[:, d:2 * d].astype(F32))
    merged = (gate_a * jnp.dot(ya_ref[...], wa_ref[...], preferred_element_type=F32)
              + gate_b * jnp.dot(yb_ref[...], wb_ref[...], preferred_element_type=F32))
    x1_ref[...] = x_ref[...] + mod_ref[2:3, :] * _mm(merged, wm_ref[...])
    h = _norm_mod(x1_ref[...], g_ref[...], mod_ref[3:4, :], mod_ref[4:5, :]).astype(BF16)
    hidden = wo_ref.shape[0]
    for j in range(hidden // FFN_CHUNK):
        js = slice(j * FFN_CHUNK, (j + 1) * FFN_CHUNK)
        gate = jnp.dot(h, wi_ref[:, js], preferred_element_type=F32)
        up = jnp.dot(h, wi_ref[:, hidden + j * FFN_CHUNK:hidden + (j + 1) * FFN_CHUNK],
                     preferred_element_type=F32)
        act = (_silu(gate) * up).astype(BF16)
        part = jnp.dot(act, wo_ref[js, :], preferred_element_type=F32)
        if j == 0:
            acc_ref[...] = part
        else:
            acc_ref[...] += part
    out = x1_ref[...] + mod_ref[5:6, :] * acc_ref[...]
    if final:
        out = out * lax.rsqrt(jnp.mean(out * out, axis=-1, keepdims=True) + EPS) * fg_ref[...]
    o_ref[...] = out


def _mix_ffn(x2, ya, yb, gates, mod_l, gain, final_g, wa, wb, wm, wi, wo, layer, rows_per_batch, final):
    m, d = x2.shape
    tm = ROW_TILE
    tpb = rows_per_batch // tm
    hidden = wo.shape[1]
    row_spec = lambda w: pl.BlockSpec((tm, w), lambda i: (i, 0))
    return pl.pallas_call(
        functools.partial(_mix_ffn_kernel, final=final),
        grid=(m // tm,),
        in_specs=[row_spec(d), row_spec(A_WIDTH), row_spec(B_WIDTH), row_spec(2 * d),
                  pl.BlockSpec((None, N_MOD, d), lambda i: (i // tpb, 0, 0)),
                  _resident((None, 1, d), lambda i: (layer, 0, 0)),
                  _resident((1, d), lambda i: (0, 0)),
                  _resident((None, A_WIDTH, d), lambda i: (layer, 0, 0)),
                  _resident((None, B_WIDTH, d), lambda i: (layer, 0, 0)),
                  _resident((None, d, d), lambda i: (layer, 0, 0)),
                  _resident((None, d, 2 * hidden), lambda i: (layer, 0, 0)),
                  _resident((None, hidden, d), lambda i: (layer, 0, 0))],
        out_specs=row_spec(d),
        out_shape=jax.ShapeDtypeStruct((m, d), F32),
        scratch_shapes=[pltpu.VMEM((tm, d), F32), pltpu.VMEM((tm, d), F32)],
        compiler_params=_params("arbitrary"),
        name="merge_swiglu",
    )(x2, ya, yb, gates, mod_l, gain, final_g, wa, wb, wm, wi, wo)


def _bias_rows(rel_table):
    t = rel_table.astype(F32)
    lo = A_MAX_REL - (CHUNK - 1)
    ramp = t[..., lo:][..., ::-1]
    n_far = CHUNK + A_BAND - 1 - ramp.shape[-1]
    far = jnp.broadcast_to(t[..., -1:], t.shape[:-1] + (n_far,))
    pad = jnp.zeros(t.shape[:-1] + (BIAS_ROW - (CHUNK + A_BAND - 1),), F32)
    return jnp.concatenate([far, ramp, pad], axis=-1)


def kernel(x, c, w_ada, b_ada, norm1_g, norm2_g, w_in, rel_table, w_conv, a_log, dt_bias, gdn_norm_g,
           w_branch_a, w_branch_b, w_out, w_ffn_in, w_ffn_out, final_g):
    bsz, seq, d = x.shape
    depth = w_ada.shape[0]
    assert seq % ROW_TILE == 0 and d % LANES == 0

    mod = _modulation(c, w_ada, b_ada)
    n_main = 3 * A_WIDTH + 4 * B_WIDTH
    w_all = jnp.pad(w_in, ((0, 0), (0, 0), (0, -w_in.shape[-1] % LANES))).astype(BF16)
    w_gates, w_ba = _tail_weights(w_all, n_main, d)
    wi, wo_ffn = w_ffn_in.astype(BF16), w_ffn_out.astype(BF16)
    wa, wb, wo = w_branch_a.astype(BF16), w_branch_b.astype(BF16), w_out.astype(BF16)
    bias_rows = _bias_rows(rel_table)
    lane_pad = jnp.zeros((depth, LANES - 2 * B_HEADS), F32)
    zeros_h = jnp.zeros((depth, B_HEADS), F32)
    head_params = jnp.stack([jnp.concatenate([zeros_h, a_log, lane_pad], axis=-1),
                             jnp.concatenate([zeros_h, dt_bias, lane_pad], axis=-1)], axis=1)
    g1 = norm1_g.reshape(depth, 1, d)
    g2 = norm2_g.reshape(depth, 1, d)
    ng = gdn_norm_g.reshape(depth, 1, B_HEAD_DIM)

    x2 = x.reshape(bsz * seq, d)
    for l in range(depth):
        ya, yb, gates = _token_mixing(x2, mod[l], g1, w_all, w_gates, w_ba, w_conv, bias_rows, head_params, ng, l,
                                      seq)
        x2 = _mix_ffn(x2, ya, yb, gates, mod[l], g2, final_g.reshape(1, d),
                      wa, wb, wo, wi, wo_ffn, l, seq, final=(l == depth - 1))
    return x2.reshape(bsz, seq, d)
```

```python
import functools

import jax
import jax.numpy as jnp
from jax import lax
from jax.experimental import pallas as pl
from jax.experimental.pallas import tpu as pltpu

F32 = jnp.float32
BF16 = jnp.bfloat16

EPS = 1e-6
CHUNK = 64
A_HEADS = 8
A_HEAD_DIM = 64
A_WIDTH = A_HEADS * A_HEAD_DIM
A_PAST_CHUNKS = 8
A_BAND = (A_PAST_CHUNKS + 1) * CHUNK
A_MAX_REL = 128
B_HEADS = 4
B_HEAD_DIM = 128
B_WIDTH = B_HEADS * B_HEAD_DIM
CONV_K = 4
N_MOD = 6
LANES = 128
SUBLANES = 8
VMEM_LIMIT = 60 * 1024 * 1024

ROW_TILE = A_PAST_CHUNKS * CHUNK
COL_TILE = 256
FFN_CHUNK = 256
ATTN_PAIRS_PER_PHASE = 4
BIAS_ROW = -(-(CHUNK + A_BAND - 1) // LANES) * LANES
MOD_COL_BLOCKS = 4


def _params(*sem):
    return pltpu.CompilerParams(dimension_semantics=sem, vmem_limit_bytes=VMEM_LIMIT)


def _resident(block_shape, index_map):
    return pl.BlockSpec(block_shape, index_map, pipeline_mode=pl.Buffered(1))


def _mm(a, b):
    return jnp.dot(a.astype(BF16), b.astype(BF16), preferred_element_type=F32)


def _mm_nt(a, b):
    return lax.dot_general(a.astype(BF16), b.astype(BF16), (((1,), (1,)), ((), ())),
                           preferred_element_type=F32)


def _rows(x, g, n=CHUNK):
    return x[g * n:(g + 1) * n]


def _bmm(a, b, groups, rows_a=CHUNK, rows_b=CHUNK):
    return jnp.concatenate([_mm(_rows(a, g, rows_a), _rows(b, g, rows_b)) for g in range(groups)], axis=0)


def _sigmoid(x):
    return 1.0 / (1.0 + jnp.exp(-x))


def _silu(x):
    return x * _sigmoid(x)


def _norm_mod(x, gain, shift, scale):
    ms = jnp.mean(x * x, axis=-1, keepdims=True)
    return x * lax.rsqrt(ms + EPS) * (gain * (1.0 + scale)) + shift


def _lane_bcast(x, lane, width):
    return jnp.broadcast_to(x[:, lane:lane + 1], (x.shape[0], width))


def _interleave(*lists):
    tagged = sorted(((i + 0.5) / len(l), n, i) for n, l in enumerate(lists) for i in range(len(l)))
    return [lists[n][i] for _, n, i in tagged]


def _mod_kernel(c_ref, w_ref, b_ref, o_ref):
    cond = _silu(c_ref[...])
    o_ref[...] = _mm(cond, w_ref[...]) + b_ref[...]


def _modulation(c, w_ada, b_ada):
    depth, d, n = w_ada.shape
    bsz = c.shape[0]
    rows = -(-bsz // SUBLANES) * SUBLANES
    c_pad = jnp.zeros((rows, d), F32).at[:bsz].set(c)
    tn = n // MOD_COL_BLOCKS
    out = pl.pallas_call(
        _mod_kernel,
        grid=(depth, n // tn),
        in_specs=[
            pl.BlockSpec((rows, d), lambda l, j: (0, 0)),
            pl.BlockSpec((None, d, tn), lambda l, j: (l, 0, j)),
            pl.BlockSpec((None, 1, tn), lambda l, j: (l, 0, j)),
        ],
        out_specs=pl.BlockSpec((None, rows, tn), lambda l, j: (l, 0, j)),
        out_shape=jax.ShapeDtypeStruct((depth, rows, n), F32),
        compiler_params=_params("arbitrary", "arbitrary"),
        name="adaln_modulation",
    )(c_pad, w_ada, b_ada.reshape(depth, 1, n))
    return out[:, :bsz].reshape(depth, bsz, N_MOD, d)


def _tail_kernel(a_ref, b_ref, wg_ref, wba_ref):
    n_ba = 2 * B_HEADS
    wa = a_ref.shape[1]
    wba_ref[...] = a_ref[:, 0:LANES]
    wg_ref[...] = jnp.concatenate([a_ref[:, n_ba:wa], b_ref[:, 0:wg_ref.shape[1] - (wa - n_ba)]], axis=1)


def _tail_weights(w_all, n_main, d):
    depth, _, n_all = w_all.shape
    win = n_main // 2
    n_gates = 2 * d
    assert win % LANES == 0 and 2 * win >= 2 * B_HEADS + n_gates and n_all == n_main + 2 * B_HEADS + n_gates
    return pl.pallas_call(
        _tail_kernel,
        grid=(depth,),
        in_specs=[pl.BlockSpec((None, d, win), lambda l: (l, 0, n_main // win)),
                  pl.BlockSpec((None, d, win), lambda l: (l, 0, n_main // win + 1))],
        out_specs=[pl.BlockSpec((None, d, n_gates), lambda l: (l, 0, 0)),
                   pl.BlockSpec((None, d, LANES), lambda l: (l, 0, 0))],
        out_shape=[jax.ShapeDtypeStruct((depth, d, n_gates), BF16), jax.ShapeDtypeStruct((depth, d, LANES), BF16)],
        compiler_params=_params("arbitrary"),
        name="gate_weight_layout",
    )(w_all, w_all)


def _attention_jobs(q_scr, k_scr, v_scr, bias_scr, ya_ref, first):
    tm = q_scr.shape[0]
    pairs = A_WIDTH // LANES
    heads_per_pair = LANES // A_HEAD_DIM
    past = A_PAST_CHUNKS * CHUNK
    lane = lax.broadcasted_iota(jnp.int32, (CHUNK, LANES), 1)
    col = lax.broadcasted_iota(jnp.int32, (1, A_BAND), 1)
    live = {}

    def scores(r0, p0):
        base = r0 + (tm - past)
        parts = []
        for p in range(p0, p0 + ATTN_PAIRS_PER_PHASE):
            cs = slice(p * LANES, (p + 1) * LANES)
            q2 = q_scr[r0:r0 + CHUNK, cs]
            zero = jnp.zeros_like(q2)
            qs = jnp.concatenate(
                [jnp.where((lane >= hh * A_HEAD_DIM) & (lane < (hh + 1) * A_HEAD_DIM), q2, zero)
                 for hh in range(heads_per_pair)], axis=0)
            parts.append(_mm_nt(qs, k_scr[base:base + A_BAND, cs]))
        valid = jnp.logical_or(col + base >= tm, jnp.logical_not(first))
        bias = bias_scr[p0 * LANES:(p0 + ATTN_PAIRS_PER_PHASE) * LANES, :]
        live["s"] = jnp.concatenate(parts, axis=0) + bias + jnp.where(valid, 0.0, -1e30)

    def softmax(r0, p0):
        s = live.pop("s")
        e = jnp.exp(s - jnp.max(s, axis=-1, keepdims=True))
        live["p"] = (e.astype(BF16), 1.0 / jnp.sum(e, axis=-1, keepdims=True))

    def output(r0, p0):
        pb, rinv = live.pop("p")
        base = r0 + (tm - past)
        for n, p in enumerate(range(p0, p0 + ATTN_PAIRS_PER_PHASE)):
            cs = slice(p * LANES, (p + 1) * LANES)
            rs = slice(n * LANES, (n + 1) * LANES)
            o = _mm(pb[rs], v_scr[base:base + A_BAND, cs]) * rinv[rs]
            o2 = jnp.where(lane < A_HEAD_DIM, o[:CHUNK], o[CHUNK:])
            ya_ref[r0:r0 + CHUNK, cs] = o2.astype(ya_ref.dtype)

    return [functools.partial(f, r0, p0) for r0 in range(0, tm, CHUNK)
            for p0 in range(0, pairs, ATTN_PAIRS_PER_PHASE) for f in (scores, softmax, output)]


def _gdn_jobs(x_ref, z_ref, ba_ref, hp_ref, ng_ref, o_ref,
              u_scr, w_scr, qd_scr, qk_scr, kdt_scr, gl_scr, o_scr, state):
    rows = x_ref.shape[0]
    groups = rows // CHUNK
    heads = range(B_HEADS)
    v_ = {}

    def pair_mask(b):
        ri, ci = v_["ri"], v_["ci"]
        return ((ri ^ ci) < 2 * b) & ((ri & b) != 0) & ((ci & b) == 0)

    def setup():
        ri = lax.broadcasted_iota(jnp.int32, (rows, CHUNK), 0) & (CHUNK - 1)
        ci = lax.broadcasted_iota(jnp.int32, (rows, CHUNK), 1)
        r1 = lax.broadcasted_iota(jnp.int32, (CHUNK, CHUNK), 0)
        c1 = lax.broadcasted_iota(jnp.int32, (CHUNK, CHUNK), 1)
        tril = jnp.where(r1 >= c1, 1.0, 0.0).astype(F32)
        v_.update(ri=ri, ci=ci, ts=[], ps=[], rhss=[])
        ba = ba_ref[...]
        sp = ba + hp_ref[1:2, :]
        softplus = jnp.maximum(sp, 0.0) + jnp.log(1.0 + jnp.exp(-jnp.abs(sp)))
        g_all = -jnp.exp(hp_ref[0:1, :]) * softplus
        gc_all = jnp.concatenate(
            [jnp.dot(tril, _rows(g_all, g), preferred_element_type=F32, precision=lax.Precision.HIGHEST)
             for g in range(groups)], axis=0)
        gc_last = jnp.concatenate(
            [jnp.broadcast_to(gc_all[(g + 1) * CHUNK - 1:(g + 1) * CHUNK, :], (CHUNK, LANES))
             for g in range(groups)], axis=0)
        v_.update(beta_all=_sigmoid(ba), g_all=g_all, gc_all=gc_all, e_gc=jnp.exp(gc_all),
                  e_rev=jnp.exp(gc_last - gc_all), e_last=jnp.exp(gc_last))

    def prep(h):
        ri, ci = v_["ri"], v_["ci"]
        q = x_ref[:, h * B_HEAD_DIM:(h + 1) * B_HEAD_DIM]
        k = x_ref[:, B_WIDTH + h * B_HEAD_DIM:B_WIDTH + (h + 1) * B_HEAD_DIM]
        v = x_ref[:, 2 * B_WIDTH + h * B_HEAD_DIM:2 * B_WIDTH + (h + 1) * B_HEAD_DIM]
        q = q * (lax.rsqrt(jnp.sum(q * q, axis=-1, keepdims=True) + EPS) * (B_HEAD_DIM ** -0.5))
        k = k * lax.rsqrt(jnp.sum(k * k, axis=-1, keepdims=True) + EPS)
        beta = _lane_bcast(v_["beta_all"], h, B_HEAD_DIM)
        egc = _lane_bcast(v_["e_gc"], B_HEADS + h, B_HEAD_DIM)
        gmat = _lane_bcast(v_["g_all"], B_HEADS + h, CHUNK)
        gc_col = _lane_bcast(v_["gc_all"], B_HEADS + h, CHUNK)
        gc_row = jnp.sum(jnp.where(ri <= ci, gmat, 0.0).reshape(groups, CHUNK, CHUNK), axis=1, keepdims=True)
        gc_row = jnp.broadcast_to(gc_row, (groups, CHUNK, CHUNK)).reshape(rows, CHUNK)
        decay = jnp.exp(jnp.where(ri >= ci, gc_col - gc_row, -jnp.inf))
        kb, qb = k.astype(BF16), q.astype(BF16)
        both = [_mm_nt(jnp.concatenate([_rows(kb, g), _rows(qb, g)], axis=0), _rows(kb, g)) for g in range(groups)]
        kk = jnp.concatenate([r[:CHUNK] for r in both], axis=0)
        qk = jnp.concatenate([r[CHUNK:] for r in both], axis=0)
        a = jnp.where(ri > ci, beta[:, :CHUNK] * kk * decay, 0.0)
        eye = jnp.where(ri == ci, 1.0, 0.0).astype(F32)
        v_["ts"].append(eye - jnp.where(pair_mask(1), a, 0.0))
        v_["ps"].append(a)
        v_["rhss"].append(jnp.concatenate([v * beta, k * (beta * egc)], axis=-1))
        k_dec = k * _lane_bcast(v_["e_rev"], B_HEADS + h, B_HEAD_DIM)
        qd_scr[h] = (q * egc).astype(BF16)
        qk_scr[h] = (qk * decay).astype(BF16)
        kdt_scr[h] = jnp.concatenate([_rows(k_dec, g).T for g in range(groups)], axis=0).astype(BF16)
        e_last = _lane_bcast(v_["e_last"], B_HEADS + h, B_HEAD_DIM)
        gl_scr[h] = jnp.concatenate([e_last[g * CHUNK:g * CHUNK + SUBLANES] for g in range(groups)], axis=0)

    def level(b):
        below = pair_mask(b)
        ns = [jnp.where(below, a, 0.0) for a in v_["ps"]]
        ws = [_bmm(n, t, groups) for n, t in zip(ns, v_["ts"])]
        v_["ts"] = [t - _bmm(t, w, groups) for t, w in zip(v_["ts"], ws)]

    def solve(h):
        sol = _bmm(v_["ts"][h], v_["rhss"][h], groups)
        u_scr[h] = sol[:, :B_HEAD_DIM]
        w_scr[h] = sol[:, B_HEAD_DIM:].astype(BF16)

    def step(c):
        blk = lambda ref, h, n=CHUNK: ref[h, c * n:(c + 1) * n, :]
        s_old = [state[h] for h in heads]
        r1s = [_mm(jnp.concatenate([blk(w_scr, h), blk(qd_scr, h)], axis=0), s) for h, s in zip(heads, s_old)]
        us = [blk(u_scr, h) - r[:CHUNK] for h, r in zip(heads, r1s)]
        r2s = [_mm(jnp.concatenate([blk(qk_scr, h), blk(kdt_scr, h, B_HEAD_DIM)], axis=0), u)
               for h, u in zip(heads, us)]
        for h, s, ra, rb in zip(heads, s_old, r1s, r2s):
            gl = blk(gl_scr, h, SUBLANES)
            state[h] = s * jnp.concatenate([gl] * (B_HEAD_DIM // SUBLANES), axis=0) + rb[CHUNK:]
            o_scr[c * CHUNK:(c + 1) * CHUNK, h * B_HEAD_DIM:(h + 1) * B_HEAD_DIM] = ra[CHUNK:] + rb[:CHUNK]

    def finish(h):
        cs = slice(h * B_HEAD_DIM, (h + 1) * B_HEAD_DIM)
        o = o_scr[:, cs]
        o = o * lax.rsqrt(jnp.mean(o * o, axis=-1, keepdims=True) + EPS) * ng_ref[...] * _silu(z_ref[:, cs])
        o_ref[:, cs] = o.astype(o_ref.dtype)

    levels = []
    b = 2
    while b < CHUNK:
        levels.append(functools.partial(level, b))
        b *= 2
    return ([setup] + [functools.partial(prep, h) for h in heads] + levels
            + [functools.partial(solve, h) for h in heads] + [functools.partial(step, c) for c in range(groups)]
            + [functools.partial(finish, h) for h in heads])


def _mixing_kernel(x_ref, mod_ref, g_ref, wm_ref, wg_ref, wba_ref, wc_ref, brow_ref, hp_ref, ng_ref,
                   ya_ref, yb_ref, og_ref,
                   xext, q_scr, k_scr, v_scr, bias_scr, qkv_scr, z_scr, ba_scr,
                   u_scr, w_scr, qd_scr, qk_scr, kdt_scr, gl_scr, o_scr, state,
                   *, tiles_per_batch):
    tm = x_ref.shape[0]
    a3, b3 = 3 * A_WIDTH, 3 * B_WIDTH
    first = pl.program_id(0) % tiles_per_batch == 0

    @pl.when(pl.program_id(0) == 0)
    def _():
        for hd in range(A_HEADS):
            tiled = jnp.broadcast_to(brow_ref[hd:hd + 1, :], (CHUNK, BIAS_ROW))
            rolled = pltpu.roll(tiled, BIAS_ROW - (CHUNK - 1), 1, stride=1, stride_axis=0)
            bias_scr[hd * CHUNK:(hd + 1) * CHUNK, :] = rolled[:, :A_BAND]
        k_scr[...] = jnp.zeros_like(k_scr)
        v_scr[...] = jnp.zeros_like(v_scr)

    @pl.when(first)
    def _():
        xext[0:SUBLANES, :] = jnp.zeros((SUBLANES, b3), F32)
        state[...] = jnp.zeros_like(state)

    k_scr[0:tm, :] = k_scr[tm:2 * tm, :]
    v_scr[0:tm, :] = v_scr[tm:2 * tm, :]

    h = _norm_mod(x_ref[...], g_ref[...], mod_ref[0:1, :], mod_ref[1:2, :]).astype(BF16)

    def proj(w_ref, c0, cw):
        return jnp.dot(h, w_ref[:, c0:c0 + cw], preferred_element_type=F32)

    scale = A_HEAD_DIM ** -0.5
    for c0 in range(0, A_WIDTH, COL_TILE):
        q_scr[:, c0:c0 + COL_TILE] = proj(wm_ref, c0, COL_TILE).astype(BF16) * scale
        k_scr[tm:2 * tm, c0:c0 + COL_TILE] = proj(wm_ref, A_WIDTH + c0, COL_TILE).astype(BF16)
        v_scr[tm:2 * tm, c0:c0 + COL_TILE] = proj(wm_ref, 2 * A_WIDTH + c0, COL_TILE).astype(BF16)

    def conv_tile(c0):
        full = xext[:, c0:c0 + LANES]
        acc = wc_ref[CONV_K - 1:CONV_K, c0:c0 + LANES] * full[SUBLANES:]
        for s in range(1, CONV_K):
            shifted = pltpu.roll(full, s, 0)[SUBLANES:]
            acc = acc + wc_ref[CONV_K - 1 - s:CONV_K - s, c0:c0 + LANES] * shifted
        qkv_scr[:, c0:c0 + LANES] = _silu(acc)

    def raw_job(r0):
        xext[SUBLANES:SUBLANES + tm, r0:r0 + COL_TILE] = proj(wm_ref, a3 + r0, COL_TILE)
        for c0 in range(r0, r0 + COL_TILE, LANES):
            conv_tile(c0)

    def plain_job(o_ref, w_ref, wc0, oc0):
        cw = min(COL_TILE, o_ref.shape[1])
        o_ref[:, oc0:oc0 + cw] = proj(w_ref, wc0, cw).astype(o_ref.dtype)

    attn = _attention_jobs(q_scr, k_scr, v_scr, bias_scr, ya_ref, first)
    gdn = _gdn_jobs(qkv_scr, z_scr, ba_scr, hp_ref, ng_ref, yb_ref,
                    u_scr, w_scr, qd_scr, qk_scr, kdt_scr, gl_scr, o_scr, state)
    feed = [functools.partial(raw_job, r0) for r0 in range(0, b3, COL_TILE)]
    feed += [functools.partial(plain_job, z_scr, wm_ref, a3 + b3 + c0, c0) for c0 in range(0, B_WIDTH, COL_TILE)]
    feed += [functools.partial(plain_job, ba_scr, wba_ref, 0, 0)]
    gates = [functools.partial(plain_job, og_ref, wg_ref, c0, c0) for c0 in range(0, og_ref.shape[1], COL_TILE)]
    n_early = len(attn) * 2 // 3
    for job in _interleave(feed, attn[:n_early]) + _interleave(gates, gdn, attn[n_early:]):
        job()
    xext[0:SUBLANES, :] = xext[tm:tm + SUBLANES, :]


def _token_mixing(x2, mod_l, gain, w_all, w_gates, w_ba, w_conv, bias_rows, head_params, norm_g, layer,
                  rows_per_batch):
    m, d = x2.shape
    tm = ROW_TILE
    tpb = rows_per_batch // tm
    groups = tm // CHUNK
    widths = (A_WIDTH, B_WIDTH, 2 * d)
    dtypes = (BF16, BF16, BF16)
    row_spec = lambda w: pl.BlockSpec((tm, w), lambda i: (i, 0))
    head_scr = lambda r, w, t: pltpu.VMEM((B_HEADS, r, w), t)
    return pl.pallas_call(
        functools.partial(_mixing_kernel, tiles_per_batch=tpb),
        grid=(m // tm,),
        in_specs=[
            row_spec(d),
            pl.BlockSpec((None, N_MOD, d), lambda i: (i // tpb, 0, 0)),
            _resident((None, 1, d), lambda i: (layer, 0, 0)),
            _resident((None, d, 3 * A_WIDTH + 4 * B_WIDTH), lambda i: (layer, 0, 0)),
            _resident((None, d, 2 * d), lambda i: (layer, 0, 0)),
            _resident((None, d, LANES), lambda i: (layer, 0, 0)),
            _resident((None, CONV_K, 3 * B_WIDTH), lambda i: (layer, 0, 0)),
            _resident((None, A_HEADS, BIAS_ROW), lambda i: (layer, 0, 0)),
            _resident((None, 2, LANES), lambda i: (layer, 0, 0)),
            _resident((None, 1, B_HEAD_DIM), lambda i: (layer, 0, 0)),
        ],
        out_specs=[row_spec(w) for w in widths],
        out_shape=[jax.ShapeDtypeStruct((m, w), t) for w, t in zip(widths, dtypes)],
        scratch_shapes=[pltpu.VMEM((tm + SUBLANES, 3 * B_WIDTH), F32),
                        pltpu.VMEM((tm, A_WIDTH), BF16),
                        pltpu.VMEM((2 * tm, A_WIDTH), BF16), pltpu.VMEM((2 * tm, A_WIDTH), BF16),
                        pltpu.VMEM((A_HEADS * CHUNK, A_BAND), F32),
                        pltpu.VMEM((tm, 3 * B_WIDTH), F32), pltpu.VMEM((tm, B_WIDTH), F32),
                        pltpu.VMEM((tm, LANES), F32),
                        head_scr(tm, B_HEAD_DIM, F32), head_scr(tm, B_HEAD_DIM, BF16),
                        head_scr(tm, B_HEAD_DIM, BF16), head_scr(tm, CHUNK, BF16),
                        head_scr(groups * B_HEAD_DIM, CHUNK, BF16), head_scr(groups * SUBLANES, B_HEAD_DIM, F32),
                        pltpu.VMEM((tm, B_WIDTH), F32),
                        pltpu.VMEM((B_HEADS, B_HEAD_DIM, B_HEAD_DIM), F32)],
        compiler_params=_params("arbitrary"),
        name="token_mixing",
    )(x2, mod_l, gain, w_all, w_gates, w_ba, w_conv, bias_rows, head_params, norm_g)


def _mix_ffn_kernel(x_ref, ya_ref, yb_ref, gt_ref, mod_ref, g_ref, fg_ref, wa_ref, wb_ref, wm_ref, wi_ref, wo_ref,
                    o_ref, x1_ref, acc_ref, *, final):
    d = x_ref.shape[-1]
    gate_a = _sigmoid(gt_ref[:, 0:d].astype(F32))
    gate_b = _sigmoid(gt_ref[:, d:2 * d].astype(F32))
    merged = (gate_a * jnp.dot(ya_ref[...], wa_ref[...], preferred_element_type=F32)
              + gate_b * jnp.dot(yb_ref[...], wb_ref[...], preferred_element_type=F32))
    x1_ref[...] = x_ref[...] + mod_ref[2:3, :] * _mm(merged, wm_ref[...])
    h = _norm_mod(x1_ref[...], g_ref[...], mod_ref[3:4, :], mod_ref[4:5, :]).astype(BF16)
    hidden = wo_ref.shape[0]
    for j in range(hidden // FFN_CHUNK):
        js = slice(j * FFN_CHUNK, (j + 1) * FFN_CHUNK)
        gate = jnp.dot(h, wi_ref[:, js], preferred_element_type=F32)
        up = jnp.dot(h, wi_ref[:, hidden + j * FFN_CHUNK:hidden + (j + 1) * FFN_CHUNK],
                     preferred_element_type=F32)
        act = (_silu(gate) * up).astype(BF16)
        part = jnp.dot(act, wo_ref[js, :], preferred_element_type=F32)
        if j == 0:
            acc_ref[...] = part
        else:
            acc_ref[...] += part
    out = x1_ref[...] + mod_ref[5:6, :] * acc_ref[...]
    if final:
        out = out * lax.rsqrt(jnp.mean(out * out, axis=-1, keepdims=True) + EPS) * fg_ref[...]
    o_ref[...] = out


def _mix_ffn(x2, ya, yb, gates, mod_l, gain, final_g, wa, wb, wm, wi, wo, layer, rows_per_batch, final):
    m, d = x2.shape
    tm = ROW_TILE
    tpb = rows_per_batch // tm
    hidden = wo.shape[1]
    row_spec = lambda w: pl.BlockSpec((tm, w), lambda i: (i, 0))
    return pl.pallas_call(
        functools.partial(_mix_ffn_kernel, final=final),
        grid=(m // tm,),
        in_specs=[row_spec(d), row_spec(A_WIDTH), row_spec(B_WIDTH), row_spec(2 * d),
                  pl.BlockSpec((None, N_MOD, d), lambda i: (i // tpb, 0, 0)),
                  _resident((None, 1, d), lambda i: (layer, 0, 0)),
                  _resident((1, d), lambda i: (0, 0)),
                  _resident((None, A_WIDTH, d), lambda i: (layer, 0, 0)),
                  _resident((None, B_WIDTH, d), lambda i: (layer, 0, 0)),
                  _resident((None, d, d), lambda i: (layer, 0, 0)),
                  _resident((None, d, 2 * hidden), lambda i: (layer, 0, 0)),
                  _resident((None, hidden, d), lambda i: (layer, 0, 0))],
        out_specs=row_spec(d),
        out_shape=jax.ShapeDtypeStruct((m, d), F32),
        scratch_shapes=[pltpu.VMEM((tm, d), F32), pltpu.VMEM((tm, d), F32)],
        compiler_params=_params("arbitrary"),
        name="merge_swiglu",
    )(x2, ya, yb, gates, mod_l, gain, final_g, wa, wb, wm, wi, wo)


def _bias_rows(rel_table):
    t = rel_table.astype(F32)
    lo = A_MAX_REL - (CHUNK - 1)
    ramp = t[..., lo:][..., ::-1]
    n_far = CHUNK + A_BAND - 1 - ramp.shape[-1]
    far = jnp.broadcast_to(t[..., -1:], t.shape[:-1] + (n_far,))
    pad = jnp.zeros(t.shape[:-1] + (BIAS_ROW - (CHUNK + A_BAND - 1),), F32)
    return jnp.concatenate([far, ramp, pad], axis=-1)


def kernel(x, c, w_ada, b_ada, norm1_g, norm2_g, w_in, rel_table, w_conv, a_log, dt_bias, gdn_norm_g,
           w_branch_a, w_branch_b, w_out, w_ffn_in, w_ffn_out, final_g):
    bsz, seq, d = x.shape
    depth = w_ada.shape[0]
    assert seq % ROW_TILE == 0 and d % LANES == 0

    mod = _modulation(c, w_ada, b_ada)
    n_main = 3 * A_WIDTH + 4 * B_WIDTH
    w_all = w_in.astype(BF16)
    w_gates, w_ba = _tail_weights(w_all, n_main, d)
    wi, wo_ffn = w_ffn_in.astype(BF16), w_ffn_out.astype(BF16)
    wa, wb, wo = w_branch_a.astype(BF16), w_branch_b.astype(BF16), w_out.astype(BF16)
    bias_rows = _bias_rows(rel_table)
    lane_pad = jnp.zeros((depth, LANES - 2 * B_HEADS), F32)
    zeros_h = jnp.zeros((depth, B_HEADS), F32)
    head_params = jnp.stack([jnp.concatenate([zeros_h, a_log, lane_pad], axis=-1),
                             jnp.concatenate([zeros_h, dt_bias, lane_pad], axis=-1)], axis=1)
    g1 = norm1_g.reshape(depth, 1, d)
    g2 = norm2_g.reshape(depth, 1, d)
    ng = gdn_norm_g.reshape(depth, 1, B_HEAD_DIM)

    x2 = x.reshape(bsz * seq, d)
    for l in range(depth):
        ya, yb, gates = _token_mixing(x2, mod[l], g1, w_all, w_gates, w_ba, w_conv, bias_rows, head_params, ng, l,
                                      seq)
        x2 = _mix_ffn(x2, ya, yb, gates, mod[l], g2, final_g.reshape(1, d),
                      wa, wb, wo, wi, wo_ffn, l, seq, final=(l == depth - 1))
    return x2.reshape(bsz, seq, d)
```
